```python
import math
import jax
import jax.numpy as jnp
from jax import lax
import numpy as np

D_MODEL = 1024
BATCH = 2
SEQ = 8192
DEPTH = 2
DEC_BATCH = 128
DEC_SEQ = 1
PAST_LEN = 16384
PAGE_SIZE = 128

N_EVEN = (DEPTH + 1) // 2
N_ODD = DEPTH // 2
QBLK = 128
NORM_EPS = 1e-6
MLA_HEADS = 8
MLA_Q_LORA = 384
MLA_KV_LORA = 256
MLA_NOPE = 64
MLA_ROPE = 32
MLA_V = 64
ROPE_BASE = 10000.0
SB_HEADS = 8
SB_KV_HEADS = 2
SB_HD = 64
DF_HEADS = 4
DF_KV_HEADS = 2
DF_HD = 64
RW_HEADS = 8
RW_HS = 64
RW_DIM = RW_HEADS * RW_HS
RW_W_LORA = 64
RW_A_LORA = 64
RW_G_LORA = 128
RW_LN_EPS = 64e-5
N_GROUPS = 4
EXPERTS_PER_GROUP = 8
N_EXPERTS = N_GROUPS * EXPERTS_PER_GROUP
TOPK_IN_GROUP = 2
EXPERT_HIDDEN = 256
MLA_IN = MLA_Q_LORA + MLA_KV_LORA + MLA_ROPE
SB_IN = SB_HEADS * SB_HD + 2 * SB_KV_HEADS * SB_HD
EVEN_IN = MLA_IN + SB_IN
EVEN_MIX = MLA_HEADS * MLA_V + SB_HEADS * SB_HD
DF_IN = DF_HEADS * 2 * DF_HD + 2 * DF_KV_HEADS * 2 * DF_HD
RW_IN = 3 * RW_DIM + RW_W_LORA + RW_A_LORA + RW_G_LORA
ODD_IN = DF_IN + RW_IN
ODD_MIX = DF_HEADS * 2 * DF_HD + RW_DIM

kernel_name = 'hybrid_mla_sb_diff_rwkv7_hmoe_step'


def _rmsnorm(x, g):
    xf = x.astype(jnp.float32)
    y = xf * lax.rsqrt(jnp.mean(xf * xf, axis=-1, keepdims=True) + NORM_EPS)
    return (y * g.astype(jnp.float32)).astype(x.dtype)


def _split(x, sizes):
    return jnp.split(x, np.cumsum(sizes)[:-1].tolist(), axis=-1)


def _rope(x, pos):
    half = x.shape[-1] // 2
    inv = ROPE_BASE ** (-jnp.arange(half, dtype=jnp.float32) / half)
    ang = pos.astype(jnp.float32)[:, None] * inv[None, :]
    if x.ndim == 4:
        ang = ang[:, None, :]
    cos, sin = jnp.cos(ang), jnp.sin(ang)
    xf = x.astype(jnp.float32)
    x1, x2 = xf[..., :half], xf[..., half:]
    return jnp.concatenate([x1 * cos - x2 * sin, x1 * sin + x2 * cos], axis=-1).astype(x.dtype)


def _gather_pages(pool, page_table):
    g = pool[page_table]
    return g.reshape(g.shape[0], g.shape[1] * g.shape[2], *g.shape[3:])


def _sweep(fn, qs, q_pos):
    t = q_pos.shape[0]
    if t <= QBLK or t % QBLK:
        return fn(qs, q_pos)
    n = t // QBLK
    split = lambda a: jnp.swapaxes(a.reshape(a.shape[0], n, QBLK, *a.shape[2:]), 0, 1)
    out = lax.map(lambda blk: fn(blk[0], blk[1]), (tuple(split(a) for a in qs), q_pos.reshape(n, QBLK)))
    out = jnp.swapaxes(out, 0, 1)
    return out.reshape(out.shape[0], t, *out.shape[3:])


def _weighted_sum(w, vals, eq):
    offs = np.cumsum([0] + [v.shape[1] for v in vals]).tolist()
    outs = [jnp.einsum(eq, w[..., offs[j]:offs[j + 1]].astype(v.dtype), v) for j, v in enumerate(vals)]
    return sum(outs[1:], outs[0])


def _mla_attend(q_abs, q_pe, q_pos, parts):
    scale = (MLA_NOPE + MLA_ROPE) ** -0.5
    s = jnp.concatenate([
        jnp.einsum('bthc,bsc->bhts', q_abs, c, preferred_element_type=jnp.float32)
        + jnp.einsum('bthr,bsr->bhts', q_pe, r, preferred_element_type=jnp.float32)
        for c, r, _ in parts], axis=-1) * scale
    k_pos = jnp.concatenate([kp for _, _, kp in parts])
    p = jax.nn.softmax(jnp.where(k_pos[None, :] <= q_pos[:, None], s, -jnp.inf), axis=-1)
    return _weighted_sum(p, [c for c, _, _ in parts], 'bhts,bsc->bthc')


def _sb_attend(q, q_pos, parts):
    z = jnp.concatenate([jnp.einsum('btkgd,bskd->bkgts', q, k, preferred_element_type=jnp.float32)
                         for k, _, _ in parts], axis=-1) * SB_HD ** -0.5
    k_pos = jnp.concatenate([kp for _, _, kp in parts])
    mask = k_pos[None, :] < q_pos[:, None]
    log_1mb = jnp.where(mask, jax.nn.log_sigmoid(-z), 0.0)
    after = lax.cumsum(log_1mb, axis=z.ndim - 1, reverse=True) - log_1mb
    w = jnp.where(mask, jnp.exp(jax.nn.log_sigmoid(z) + after), 0.0)
    out = _weighted_sum(w, [v for _, v, _ in parts], 'bkgts,bskd->btkgd')
    return out.reshape(out.shape[0], out.shape[1], -1)


def _diff_attend(q, q_pos, parts, lam):
    s = jnp.concatenate([jnp.einsum('btkgcd,bskcd->bkgcts', q, k, preferred_element_type=jnp.float32)
                         for k, _, _ in parts], axis=-1) * DF_HD ** -0.5
    k_pos = jnp.concatenate([kp for _, _, kp in parts])
    p = jax.nn.softmax(jnp.where(k_pos[None, :] <= q_pos[:, None], s, -jnp.inf), axis=-1)
    a = p[:, :, :, 0] - lam * p[:, :, :, 1]
    return _weighted_sum(a, [v for _, v, _ in parts], 'bkgts,bske->btkge')


def _even_mixer(hn, pos, past, w_in, q_norm, w_uq, kv_norm, w_uk, w_uv, w_out):
    b, t, _ = hn.shape
    proj = hn @ w_in
    q_lat, kv_lat, kpe_raw, sq, sk, sv = _split(
        proj, [MLA_Q_LORA, MLA_KV_LORA, MLA_ROPE, SB_HEADS * SB_HD, SB_KV_HEADS * SB_HD, SB_KV_HEADS * SB_HD])
    q = (_rmsnorm(q_lat, q_norm) @ w_uq).reshape(b, t, MLA_HEADS, MLA_NOPE + MLA_ROPE)
    q_nope, q_pe = q[..., :MLA_NOPE], _rope(q[..., MLA_NOPE:], pos)
    ckv = _rmsnorm(kv_lat, kv_norm)
    kpe = _rope(kpe_raw, pos)
    q_abs = jnp.einsum('bthn,chn->bthc', q_nope, w_uk)
    sq = sq.reshape(b, t, SB_KV_HEADS, SB_HEADS // SB_KV_HEADS, SB_HD)
    sk = sk.reshape(b, t, SB_KV_HEADS, SB_HD)
    sv = sv.reshape(b, t, SB_KV_HEADS, SB_HD)
    if past is None:
        mla_parts = [(ckv, kpe, pos)]
        sb_parts = [(sk, sv, pos)]
    else:
        p_ckv, p_kpe, p_sk, p_sv, p_pos = past
        mla_parts = [(p_ckv, p_kpe, p_pos), (ckv, kpe, pos)]
        sb_parts = [(p_sk, p_sv, p_pos), (sk, sv, pos)]
    lat = _sweep(lambda qs, qp: _mla_attend(qs[0], qs[1], qp, mla_parts), (q_abs, q_pe), pos)
    mla_out = jnp.einsum('bthc,chv->bthv', lat, w_uv).reshape(b, t, MLA_HEADS * MLA_V)
    sb_out = _sweep(lambda qs, qp: _sb_attend(qs[0], qp, sb_parts), (sq,), pos)
    y = jnp.concatenate([mla_out, sb_out], axis=-1) @ w_out
    return y, (ckv, kpe, sk, sv)


def _odd_mixer(hn, pos, past, shift_prev, s_prev, lam_init, w_in, lq1, lk1, lq2, lk2, subln,
               mu, w0, w2, a0, a2, g2, kk_scale, ka, rk, ln_w, ln_b, w_out):
    b, t, _ = hn.shape
    f32 = lambda a: a.astype(jnp.float32)
    proj = hn @ w_in
    dq, dk, dv, rw = _split(proj, [DF_HEADS * 2 * DF_HD, DF_KV_HEADS * 2 * DF_HD, DF_KV_HEADS * 2 * DF_HD, RW_IN])
    dq = dq.reshape(b, t, DF_KV_HEADS, DF_HEADS // DF_KV_HEADS, 2, DF_HD)
    dk = dk.reshape(b, t, DF_KV_HEADS, 2, DF_HD)
    dv = dv.reshape(b, t, DF_KV_HEADS, 2 * DF_HD)
    lam = jnp.exp(jnp.sum(f32(lq1) * f32(lk1))) - jnp.exp(jnp.sum(f32(lq2) * f32(lk2))) + lam_init
    if past is None:
        parts = [(dk, dv, pos)]
    else:
        p_k, p_v, p_pos = past
        parts = [(p_k, p_v, p_pos), (dk, dv, pos)]
    o = _sweep(lambda qs, qp: _diff_attend(qs[0], qp, parts, lam), (dq,), pos)
    df_out = (_rmsnorm(o, subln) * (1.0 - lam_init)).reshape(b, t, DF_HEADS * 2 * DF_HD)
    prev = jnp.concatenate([shift_prev[:, None].astype(rw.dtype), rw[:, :-1]], axis=1)
    xs = rw + (prev - rw) * mu
    r, k, v, xw, xa, xg = _split(xs, [RW_DIM, RW_DIM, RW_DIM, RW_W_LORA, RW_A_LORA, RW_G_LORA])
    r, k, v = f32(r), f32(k), f32(v)
    w_log = -jax.nn.softplus(-(f32(w0) + jnp.tanh(f32(xw)) @ f32(w2))) - 0.5
    decay = jnp.exp(-jnp.exp(w_log))
    a = jax.nn.sigmoid(f32(a0) + f32(xa) @ f32(a2))
    g = jax.nn.sigmoid(f32(xg)) @ f32(g2)
    heads = lambda z: z.reshape(b, t, RW_HEADS, RW_HS)
    kk = heads(k * f32(kk_scale))
    kk = kk * lax.rsqrt(jnp.maximum(jnp.sum(kk * kk, axis=-1, keepdims=True), 1e-24))
    k = k * (1.0 + (a - 1.0) * f32(ka))
    r, decay, k, v, a = heads(r), heads(decay), heads(k), heads(v), heads(a)

    def step(S, inp):
        r_t, w_t, k_t, v_t, kk_t, a_t = inp
        sa = jnp.einsum('bhij,bhj->bhi', S, -kk_t)
        S = (S * w_t[:, :, None, :] + sa[..., None] * (kk_t * a_t)[:, :, None, :]
             + v_t[..., None] * k_t[:, :, None, :])
        return S, jnp.einsum('bhij,bhj->bhi', S, r_t)

    seq_in = tuple(jnp.moveaxis(z, 1, 0) for z in (r, decay, k, v, kk, a))
    s_fin, yr = lax.scan(step, f32(s_prev), seq_in)
    yr = jnp.moveaxis(yr, 0, 1)
    mean = jnp.mean(yr, axis=-1, keepdims=True)
    var = jnp.mean(jnp.square(yr - mean), axis=-1, keepdims=True)
    yr = ((yr - mean) * lax.rsqrt(var + RW_LN_EPS)).reshape(b, t, RW_DIM) * f32(ln_w) + f32(ln_b)
    bonus = (jnp.sum(r * k * f32(rk), axis=-1, keepdims=True) * v).reshape(b, t, RW_DIM)
    rw_out = ((yr + bonus) * g).astype(hn.dtype)
    y = jnp.concatenate([df_out, rw_out], axis=-1) @ w_out
    return y, (dk, dv), s_fin, rw[:, -1]


def _hier_moe(x, w_group, b_group, w_router, b_router, w_gate, w_up, w_down):
    b, t, d = x.shape
    xf = x.reshape(b * t, d)
    gl = jnp.einsum('nd,dg->ng', xf, w_group, preferred_element_type=jnp.float32) + b_group
    gw, gsel = lax.top_k(jax.nn.softmax(gl, axis=-1), 1)
    oh_g = jax.nn.one_hot(gsel[:, 0], N_GROUPS, dtype=jnp.float32)
    el = jnp.einsum('nd,dge->nge', xf, w_router, preferred_element_type=jnp.float32) + b_router
    el = jnp.einsum('nge,ng->ne', el, oh_g)
    tv, ti = lax.top_k(el, TOPK_IN_GROUP)
    tw = jax.nn.softmax(tv, axis=-1) * gw
    eidx = gsel * EXPERTS_PER_GROUP + ti
    combine = jnp.sum(jax.nn.one_hot(eidx, N_EXPERTS, dtype=jnp.float32) * tw[..., None], axis=1)
    h = jnp.einsum('nd,edf->nef', xf, w_gate)
    u = jnp.einsum('nd,edf->nef', xf, w_up)
    act = jax.nn.silu(h) * u * combine[..., None].astype(x.dtype)
    return jnp.einsum('nef,efd->nd', act, w_down).reshape(b, t, d)


def setup_inputs(seed: int = 0) -> dict:
    key = jax.random.key(seed)
    ks = iter(jax.random.split(key, 64))

    def nrm(shape, scale=1.0):
        return jax.random.normal(next(ks), shape, jnp.float32) * scale

    def gain(shape):
        return 1.0 + nrm(shape, 0.02)

    def unif(shape, lo, hi):
        return jax.random.uniform(next(ks), shape, jnp.float32, minval=lo, maxval=hi)

    n_pages = PAST_LEN // PAGE_SIZE
    n_used = DEC_BATCH * n_pages
    n_pool = n_used + n_used // 4
    x_prompt = nrm((BATCH, SEQ, D_MODEL))
    x_sample = nrm((DEC_BATCH, DEC_SEQ, D_MODEL))
    cache_mla_ckv = nrm((N_EVEN, n_pool, PAGE_SIZE, MLA_KV_LORA))
    cache_mla_kpe = nrm((N_EVEN, n_pool, PAGE_SIZE, MLA_ROPE))
    cache_sb_k = nrm((N_EVEN, n_pool, PAGE_SIZE, SB_KV_HEADS, SB_HD))
    cache_sb_v = nrm((N_EVEN, n_pool, PAGE_SIZE, SB_KV_HEADS, SB_HD))
    cache_df_k = nrm((N_ODD, n_pool, PAGE_SIZE, DF_KV_HEADS, 2, DF_HD))
    cache_df_v = nrm((N_ODD, n_pool, PAGE_SIZE, DF_KV_HEADS, 2 * DF_HD))
    state_rw_s = nrm((N_ODD, DEC_BATCH, RW_HEADS, RW_HS, RW_HS))
    state_rw_shift = nrm((N_ODD, DEC_BATCH, RW_IN))
    page_table = jax.random.permutation(next(ks), n_pool)[:n_used].reshape(DEC_BATCH, n_pages).astype(jnp.int32)
    return {
        'x_prompt': x_prompt, 'x_sample': x_sample,
        'cache_mla_ckv': cache_mla_ckv, 'cache_mla_kpe': cache_mla_kpe,
        'cache_sb_k': cache_sb_k, 'cache_sb_v': cache_sb_v,
        'cache_df_k': cache_df_k, 'cache_df_v': cache_df_v,
        'state_rw_s': state_rw_s, 'state_rw_shift': state_rw_shift,
        'page_table': page_table,
        'attn_norm': gain((DEPTH, D_MODEL)),
        'ffn_norm': gain((DEPTH, D_MODEL)),
        'final_norm': gain((D_MODEL,)),
        'ev_w_in': nrm((N_EVEN, D_MODEL, EVEN_IN), D_MODEL ** -0.5),
        'mla_q_norm': gain((N_EVEN, MLA_Q_LORA)),
        'mla_w_uq': nrm((N_EVEN, MLA_Q_LORA, MLA_HEADS * (MLA_NOPE + MLA_ROPE)), MLA_Q_LORA ** -0.5),
        'mla_kv_norm': gain((N_EVEN, MLA_KV_LORA)),
        'mla_w_uk': nrm((N_EVEN, MLA_KV_LORA, MLA_HEADS, MLA_NOPE), MLA_KV_LORA ** -0.5),
        'mla_w_uv': nrm((N_EVEN, MLA_KV_LORA, MLA_HEADS, MLA_V), MLA_KV_LORA ** -0.5),
        'ev_w_out': nrm((N_EVEN, EVEN_MIX, D_MODEL), EVEN_MIX ** -0.5),
        'od_w_in': nrm((N_ODD, D_MODEL, ODD_IN), D_MODEL ** -0.5),
        'df_lq1': nrm((N_ODD, DF_HD), 0.1),
        'df_lk1': nrm((N_ODD, DF_HD), 0.1),
        'df_lq2': nrm((N_ODD, DF_HD), 0.1),
        'df_lk2': nrm((N_ODD, DF_HD), 0.1),
        'df_subln': gain((N_ODD, 2 * DF_HD)),
        'rw_mu': unif((N_ODD, RW_IN), 0.0, 1.0),
        'rw_w0': unif((N_ODD, RW_DIM), -4.0, 1.0),
        'rw_w2': nrm((N_ODD, RW_W_LORA, RW_DIM), 0.1 * RW_W_LORA ** -0.5),
        'rw_a0': nrm((N_ODD, RW_DIM), 0.5),
        'rw_a2': nrm((N_ODD, RW_A_LORA, RW_DIM), 0.1 * RW_A_LORA ** -0.5),
        'rw_g2': nrm((N_ODD, RW_G_LORA, RW_DIM), RW_G_LORA ** -0.5),
        'rw_kk': 1.0 + nrm((N_ODD, RW_DIM), 0.1),
        'rw_ka': 1.0 + nrm((N_ODD, RW_DIM), 0.1),
        'rw_rk': nrm((N_ODD, RW_HEADS, RW_HS), 0.1),
        'rw_ln_w': gain((N_ODD, RW_DIM)),
        'rw_ln_b': nrm((N_ODD, RW_DIM), 0.02),
        'od_w_out': nrm((N_ODD, ODD_MIX, D_MODEL), ODD_MIX ** -0.5),
        'moe_w_group': nrm((DEPTH, D_MODEL, N_GROUPS), D_MODEL ** -0.5),
        'moe_b_group': nrm((DEPTH, N_GROUPS), 0.01),
        'moe_w_router': nrm((DEPTH, D_MODEL, N_GROUPS, EXPERTS_PER_GROUP), D_MODEL ** -0.5),
        'moe_b_router': nrm((DEPTH, N_GROUPS, EXPERTS_PER_GROUP), 0.01),
        'moe_w_gate': nrm((DEPTH, N_EXPERTS, D_MODEL, EXPERT_HIDDEN), D_MODEL ** -0.5),
        'moe_w_up': nrm((DEPTH, N_EXPERTS, D_MODEL, EXPERT_HIDDEN), D_MODEL ** -0.5),
        'moe_w_down': nrm((DEPTH, N_EXPERTS, EXPERT_HIDDEN, D_MODEL), EXPERT_HIDDEN ** -0.5),
    }


def reference(x_prompt, x_sample, cache_mla_ckv, cache_mla_kpe, cache_sb_k, cache_sb_v, cache_df_k,
              cache_df_v, state_rw_s, state_rw_shift, page_table, attn_norm, ffn_norm, final_norm,
              ev_w_in, mla_q_norm, mla_w_uq, mla_kv_norm, mla_w_uk, mla_w_uv, ev_w_out,
              od_w_in, df_lq1, df_lk1, df_lq2, df_lk2, df_subln, rw_mu, rw_w0, rw_w2, rw_a0, rw_a2,
              rw_g2, rw_kk, rw_ka, rw_rk, rw_ln_w, rw_ln_b, od_w_out, moe_w_group, moe_b_group,
              moe_w_router, moe_b_router, moe_w_gate, moe_w_up, moe_w_down):
    bp, tp, _ = x_prompt.shape
    ts = x_sample.shape[1]
    pos_p = jnp.arange(tp, dtype=jnp.int32)
    pos_s = PAST_LEN + jnp.arange(ts, dtype=jnp.int32)
    pos_past = jnp.arange(PAST_LEN, dtype=jnp.int32)
    xp, xs = x_prompt, x_sample
    ckv_p, ckv_s, kpe_p, kpe_s = [], [], [], []
    sbk_p, sbk_s, sbv_p, sbv_s = [], [], [], []
    dfk_p, dfk_s, dfv_p, dfv_s = [], [], [], []
    rws_p, rws_s, rwsh_p, rwsh_s = [], [], [], []
    for layer in range(DEPTH):
        i = layer // 2
        hp = _rmsnorm(xp, attn_norm[layer])
        hs = _rmsnorm(xs, attn_norm[layer])
        if layer % 2 == 0:
            ew = (ev_w_in[i], mla_q_norm[i], mla_w_uq[i], mla_kv_norm[i], mla_w_uk[i], mla_w_uv[i], ev_w_out[i])
            mp, rp = _even_mixer(hp, pos_p, None, *ew)
            past = (_gather_pages(cache_mla_ckv[i], page_table), _gather_pages(cache_mla_kpe[i], page_table),
                    _gather_pages(cache_sb_k[i], page_table), _gather_pages(cache_sb_v[i], page_table), pos_past)
            ms, rs = _even_mixer(hs, pos_s, past, *ew)
            ckv_p.append(rp[0]); kpe_p.append(rp[1]); sbk_p.append(rp[2]); sbv_p.append(rp[3])
            ckv_s.append(rs[0]); kpe_s.append(rs[1]); sbk_s.append(rs[2]); sbv_s.append(rs[3])
        else:
            lam_init = 0.8 - 0.6 * math.exp(-0.3 * layer)
            ow = (od_w_in[i], df_lq1[i], df_lk1[i], df_lq2[i], df_lk2[i], df_subln[i], rw_mu[i], rw_w0[i],
                  rw_w2[i], rw_a0[i], rw_a2[i], rw_g2[i], rw_kk[i], rw_ka[i], rw_rk[i], rw_ln_w[i], rw_ln_b[i],
                  od_w_out[i])
            mp, kvp, sp, shp = _odd_mixer(hp, pos_p, None, jnp.zeros((bp, RW_IN), xp.dtype),
                                          jnp.zeros((bp, RW_HEADS, RW_HS, RW_HS), jnp.float32), lam_init, *ow)
            past = (_gather_pages(cache_df_k[i], page_table), _gather_pages(cache_df_v[i], page_table), pos_past)
            ms, kvs, ss, shs = _odd_mixer(hs, pos_s, past, state_rw_shift[i], state_rw_s[i], lam_init, *ow)
            dfk_p.append(kvp[0]); dfv_p.append(kvp[1]); dfk_s.append(kvs[0]); dfv_s.append(kvs[1])
            rws_p.append(sp); rws_s.append(ss); rwsh_p.append(shp); rwsh_s.append(shs)
        xp = xp + mp
        xs = xs + ms
        mw = (moe_w_group[layer], moe_b_group[layer], moe_w_router[layer], moe_b_router[layer],
              moe_w_gate[layer], moe_w_up[layer], moe_w_down[layer])
        xp = xp + _hier_moe(_rmsnorm(xp, ffn_norm[layer]), *mw)
        xs = xs + _hier_moe(_rmsnorm(xs, ffn_norm[layer]), *mw)
    y_prompt = _rmsnorm(xp, final_norm)
    y_sample = _rmsnorm(xs, final_norm)
    st = lambda rows: jnp.stack(rows, axis=0)
    return (y_prompt, y_sample, st(ckv_p), st(ckv_s), st(kpe_p), st(kpe_s), st(sbk_p), st(sbk_s),
            st(sbv_p), st(sbv_s), st(dfk_p), st(dfk_s), st(dfv_p), st(dfv_s), st(rws_p), st(rws_s),
            st(rwsh_p), st(rwsh_s))
```

```python
import functools
import math

import jax
import jax.numpy as jnp
from jax import lax
from jax.experimental import pallas as pl
from jax.experimental.pallas import tpu as pltpu

F32 = jnp.float32
BF16 = jnp.bfloat16

PAGE_SIZE = 128
NORM_EPS = 1e-6
MLA_HEADS, MLA_Q_LORA, MLA_KV_LORA, MLA_NOPE, MLA_ROPE, MLA_V = 8, 384, 256, 64, 32, 64
ROPE_BASE = 10000.0
SB_HEADS, SB_KV_HEADS, SB_HD = 8, 2, 64
DF_HEADS, DF_KV_HEADS, DF_HD = 4, 2, 64
RW_HEADS, RW_HS = 8, 64
RW_DIM = RW_HEADS * RW_HS
RW_W_LORA, RW_A_LORA, RW_G_LORA = 64, 64, 128
RW_LN_EPS = 64e-5
N_GROUPS, EXPERTS_PER_GROUP = 4, 8
N_EXPERTS = N_GROUPS * EXPERTS_PER_GROUP
LANES = 128
VMEM_LIMIT = 48 * 1024 * 1024
DECODE_PAGES_PER_STEP = 16


def _cparams(sem):
    return pltpu.CompilerParams(dimension_semantics=sem, vmem_limit_bytes=VMEM_LIMIT)


def _row_tile(n, pref):
    t = min(n, pref)
    while n % t:
        t //= 2
    return t


def _linear_kernel(*refs, norm, has_res, emit_normed):
    it = iter(refs)
    x_ref = next(it)
    g_ref = next(it) if norm else None
    w_ref = next(it)
    res_ref = next(it) if has_res else None
    o_ref = next(it)
    xn_ref = next(it) if emit_normed else None
    x = x_ref[...].astype(F32)
    if norm:
        x = x * lax.rsqrt(jnp.mean(x * x, axis=-1, keepdims=True) + NORM_EPS) * g_ref[...]
        if emit_normed:
            xn_ref[...] = x
    y = jnp.dot(x.astype(BF16), w_ref[...], preferred_element_type=F32)
    if has_res:
        y = y + res_ref[...]
    o_ref[...] = y.astype(o_ref.dtype)


def _linear(x, w, g=None, res=None, emit_normed=False, name="linear"):
    n, k = x.shape
    m = w.shape[1]
    tm = _row_tile(n, 512)
    tn = m if (m <= 2048 or emit_normed) else m // 2
    assert m % tn == 0 and (tn == m or tn % LANES == 0)
    grid = (m // tn, n // tm)
    in_specs = [pl.BlockSpec((tm, k), lambda j, i: (i, 0))]
    args = [x]
    if g is not None:
        in_specs.append(pl.BlockSpec((1, k), lambda j, i: (0, 0)))
        args.append(g.reshape(1, k).astype(F32))
    in_specs.append(pl.BlockSpec((k, tn), lambda j, i: (0, j)))
    args.append(w.astype(BF16))
    if res is not None:
        in_specs.append(pl.BlockSpec((tm, tn), lambda j, i: (i, j)))
        args.append(res)
    out_shape = [jax.ShapeDtypeStruct((n, m), F32)]
    out_specs = [pl.BlockSpec((tm, tn), lambda j, i: (i, j))]
    if emit_normed:
        out_shape.append(jax.ShapeDtypeStruct((n, k), F32))
        out_specs.append(pl.BlockSpec((tm, k), lambda j, i: (i, 0)))
    outs = pl.pallas_call(
        functools.partial(_linear_kernel, norm=g is not None, has_res=res is not None, emit_normed=emit_normed),
        out_shape=out_shape, grid=grid, in_specs=in_specs, out_specs=out_specs,
        compiler_params=_cparams(("arbitrary", "arbitrary")), name=name)(*args)
    return outs if emit_normed else outs[0]


def _outproj_kernel(*refs, diff_heads, diff_scale):
    if diff_heads:
        a0_ref, a1_ref, lam_ref, sub_ref, b_ref, wa_ref, wb_ref, x_ref, o_ref = refs
        lam = lam_ref[0, 0]
        hd = a0_ref.shape[1] // diff_heads
        y = x_ref[...] + jnp.dot(b_ref[...].astype(BF16), wb_ref[...], preferred_element_type=F32)
        for h in range(diff_heads):
            sl = slice(h * hd, (h + 1) * hd)
            o = a0_ref[:, sl] - lam * a1_ref[:, sl]
            o = o * lax.rsqrt(jnp.mean(o * o, axis=-1, keepdims=True) + NORM_EPS) * sub_ref[...] * diff_scale
            y = y + jnp.dot(o.astype(BF16), wa_ref[sl, :], preferred_element_type=F32)
    else:
        a_ref, b_ref, wa_ref, wb_ref, x_ref, o_ref = refs
        y = x_ref[...] + jnp.dot(a_ref[...].astype(BF16), wa_ref[...], preferred_element_type=F32)
        y = y + jnp.dot(b_ref[...].astype(BF16), wb_ref[...], preferred_element_type=F32)
    o_ref[...] = y


def _outproj(a, b, w, x, diff=None, name="outproj"):
    n, d = x.shape
    ka, kb = (a[0].shape[1] if diff else a.shape[1]), b.shape[1]
    tm = _row_tile(n, 512)
    row = lambda i: (i, 0)
    const = lambda i: (0, 0)
    wa, wb = w[:ka].astype(BF16), w[ka:].astype(BF16)
    if diff:
        lam, subln, scale, heads = diff
        args = [a[0], a[1], lam.reshape(1, 1).astype(F32), subln.reshape(1, -1).astype(F32), b, wa, wb, x]
        in_specs = [pl.BlockSpec((tm, ka), row), pl.BlockSpec((tm, ka), row),
                    pl.BlockSpec(memory_space=pltpu.SMEM), pl.BlockSpec((1, ka // heads), const),
                    pl.BlockSpec((tm, kb), row), pl.BlockSpec((ka, d), const), pl.BlockSpec((kb, d), const),
                    pl.BlockSpec((tm, d), row)]
        kern = functools.partial(_outproj_kernel, diff_heads=heads, diff_scale=scale)
    else:
        args = [a, b, wa, wb, x]
        in_specs = [pl.BlockSpec((tm, ka), row), pl.BlockSpec((tm, kb), row), pl.BlockSpec((ka, d), const),
                    pl.BlockSpec((kb, d), const), pl.BlockSpec((tm, d), row)]
        kern = functools.partial(_outproj_kernel, diff_heads=0, diff_scale=1.0)
    return pl.pallas_call(kern, out_shape=jax.ShapeDtypeStruct((n, d), F32), grid=(n // tm,), in_specs=in_specs,
                          out_specs=pl.BlockSpec((tm, d), row), compiler_params=_cparams(("arbitrary",)),
                          name=name)(*args)


def _qk(q, k):
    return lax.dot_general(q, k, (((1,), (1,)), ((), ())), preferred_element_type=F32)


def _softplus(z):
    return jnp.maximum(z, 0.0) + jnp.log(1.0 + jnp.exp(-jnp.abs(z)))


def _split_bf16(x):
    hi = x.astype(BF16)
    return hi, (x - hi.astype(F32)).astype(BF16)


def _attn_softmax_kernel(q_ref, k_ref, v_ref, o_ref, m_scr, l_scr, acc_scr, *, tq):
    i = pl.program_id(1)
    g, dv = q_ref.shape[1], v_ref.shape[-1]
    r = g * tq
    q = q_ref[0].reshape(r, q_ref.shape[-1])
    m_scr[...] = jnp.full(m_scr.shape, -jnp.inf, F32)
    l_scr[...] = jnp.zeros(l_scr.shape, F32)
    acc_scr[...] = jnp.zeros(acc_scr.shape, F32)

    def chunk(j, masked):
        start = pl.multiple_of(j * tq, tq)
        s = _qk(q, k_ref[0, pl.ds(start, tq), :])
        if masked:
            row = lax.broadcasted_iota(jnp.int32, (r, tq), 0) & (tq - 1)
            col = lax.broadcasted_iota(jnp.int32, (r, tq), 1)
            s = jnp.where(col <= row, s, -jnp.inf)
        m_prev = m_scr[...]
        m_new = jnp.maximum(m_prev, jnp.max(s, axis=1, keepdims=True))
        p = jnp.exp(s - m_new)
        alpha = jnp.exp(m_prev - m_new)
        l_scr[...] = alpha * l_scr[...] + jnp.sum(p, axis=1, keepdims=True)
        acc_scr[...] = alpha * acc_scr[...] + jnp.dot(p.astype(BF16), v_ref[0, pl.ds(start, tq), :],
                                                      preferred_element_type=F32)
        m_scr[...] = m_new

    def body(j, c):
        chunk(j, False)
        return c

    lax.fori_loop(0, i, body, 0)
    chunk(i, True)
    o_ref[0] = (acc_scr[...] / l_scr[...]).reshape(g, tq, dv)


def _attn_sb_kernel(q_ref, k_ref, v_ref, tri_ref, o_ref, c_scr, acc_scr, *, tq):
    i = pl.program_id(1)
    g, dv = q_ref.shape[1], v_ref.shape[-1]
    r = g * tq
    q = q_ref[0].reshape(r, q_ref.shape[-1])
    c_scr[...] = jnp.zeros(c_scr.shape, F32)
    acc_scr[...] = jnp.zeros(acc_scr.shape, F32)

    def chunk(j, masked):
        start = pl.multiple_of(j * tq, tq)
        z = _qk(q, k_ref[0, pl.ds(start, tq), :])
        sp = _softplus(z)
        if masked:
            row = lax.broadcasted_iota(jnp.int32, (r, tq), 0) & (tq - 1)
            col = lax.broadcasted_iota(jnp.int32, (r, tq), 1)
            mask = col < row
            sp = jnp.where(mask, sp, 0.0)
        hi, lo = _split_bf16(sp)
        tri = tri_ref[...]
        si = jnp.dot(hi, tri, preferred_element_type=F32) + jnp.dot(lo, tri, preferred_element_type=F32)
        w = jnp.exp(z - si - c_scr[...])
        if masked:
            w = jnp.where(mask, w, 0.0)
        acc_scr[...] += jnp.dot(w.astype(BF16), v_ref[0, pl.ds(start, tq), :], preferred_element_type=F32)
        c_scr[...] += si[:, 0:1]

    chunk(i, True)

    def body(jj, c):
        chunk(i - 1 - jj, False)
        return c

    lax.fori_loop(0, i, body, 0)
    o_ref[0] = acc_scr[...].reshape(g, tq, dv)


def _suffix_ones(n):
    return (lax.broadcasted_iota(jnp.int32, (n, n), 0) >= lax.broadcasted_iota(jnp.int32, (n, n), 1)).astype(BF16)


def _causal_attention(q, k, v, mode, tq, name):
    bh, g, t, dk = q.shape
    dv = v.shape[-1]
    kdiv, vdiv = bh // k.shape[0], bh // v.shape[0]
    tq = min(tq, t)
    assert t % tq == 0 and tq & (tq - 1) == 0
    r = g * tq
    in_specs = [pl.BlockSpec((1, g, tq, dk), lambda b, i: (b, 0, i, 0)),
                pl.BlockSpec((1, t, dk), lambda b, i: (b // kdiv, 0, 0)),
                pl.BlockSpec((1, t, dv), lambda b, i: (b // vdiv, 0, 0))]
    args = [q, k, v]
    if mode == "sb":
        in_specs.append(pl.BlockSpec((tq, tq), lambda b, i: (0, 0)))
        args.append(_suffix_ones(tq))
        kern = functools.partial(_attn_sb_kernel, tq=tq)
        scratch = [pltpu.VMEM((r, 1), F32), pltpu.VMEM((r, dv), F32)]
    else:
        kern = functools.partial(_attn_softmax_kernel, tq=tq)
        scratch = [pltpu.VMEM((r, 1), F32), pltpu.VMEM((r, 1), F32), pltpu.VMEM((r, dv), F32)]
    return pl.pallas_call(kern, out_shape=jax.ShapeDtypeStruct((bh, g, t, dv), F32), grid=(bh, t // tq),
                          in_specs=in_specs, out_specs=pl.BlockSpec((1, g, tq, dv), lambda b, i: (b, 0, i, 0)),
                          scratch_shapes=scratch, compiler_params=_cparams(("arbitrary", "arbitrary")),
                          name=name)(*args)


def _decode_softmax_kernel(pt_ref, *refs, npages, nparts, shared_v):
    del pt_ref
    it = iter(refs)
    q_refs = [next(it) for _ in range(nparts)]
    knew_refs = [next(it) for _ in range(nparts)]
    vnew_ref = None if shared_v else next(it)
    k_pages = [[next(it) for _ in range(npages)] for _ in range(nparts)]
    v_pages = None if shared_v else [next(it) for _ in range(npages)]
    o_ref = next(it)
    kb = [next(it) for _ in range(nparts)]
    vb = kb[0] if shared_v else next(it)
    m_scr, l_scr, acc_scr = next(it), next(it), next(it)
    c = pl.program_id(1)

    @pl.when(c == 0)
    def _():
        m_scr[...] = jnp.full(m_scr.shape, -jnp.inf, F32)
        l_scr[...] = jnp.zeros(l_scr.shape, F32)
        acc_scr[...] = jnp.zeros(acc_scr.shape, F32)

    for a in range(nparts):
        for p in range(npages):
            kb[a][p * PAGE_SIZE:(p + 1) * PAGE_SIZE, :] = k_pages[a][p][...].astype(BF16)
    if not shared_v:
        for p in range(npages):
            vb[p * PAGE_SIZE:(p + 1) * PAGE_SIZE, :] = v_pages[p][...].astype(BF16)
    s = _qk(q_refs[0][0], kb[0][...])
    for a in range(1, nparts):
        s = s + _qk(q_refs[a][0], kb[a][...])
    m_prev = m_scr[...]
    m_new = jnp.maximum(m_prev, jnp.max(s, axis=1, keepdims=True))
    p_ = jnp.exp(s - m_new)
    alpha = jnp.exp(m_prev - m_new)
    l_scr[...] = alpha * l_scr[...] + jnp.sum(p_, axis=1, keepdims=True)
    acc_scr[...] = alpha * acc_scr[...] + jnp.dot(p_.astype(BF16), vb[...], preferred_element_type=F32)
    m_scr[...] = m_new

    @pl.when(c == pl.num_programs(1) - 1)
    def _():
        s_new = jnp.sum(q_refs[0][0].astype(F32) * knew_refs[0][0], axis=1, keepdims=True)
        for a in range(1, nparts):
            s_new = s_new + jnp.sum(q_refs[a][0].astype(F32) * knew_refs[a][0], axis=1, keepdims=True)
        v_new = knew_refs[0][0] if shared_v else vnew_ref[0]
        m_prev = m_scr[...]
        m_fin = jnp.maximum(m_prev, s_new)
        p_new = jnp.exp(s_new - m_fin)
        alpha = jnp.exp(m_prev - m_fin)
        l_fin = alpha * l_scr[...] + p_new
        o_ref[0] = (alpha * acc_scr[...] + p_new * v_new) / l_fin


def _decode_sb_kernel(pt_ref, q_ref, *refs, npages):
    del pt_ref
    k_pages = refs[:npages]
    v_pages = refs[npages:2 * npages]
    tri_ref, o_ref, kb, vb, c_scr, acc_scr = refs[2 * npages:]
    c = pl.program_id(1)
    r = q_ref.shape[1]

    @pl.when(c == 0)
    def _():
        c_scr[...] = jnp.zeros(c_scr.shape, F32)
        acc_scr[...] = jnp.zeros(acc_scr.shape, F32)

    for p in range(npages):
        kb[p * PAGE_SIZE:(p + 1) * PAGE_SIZE, :] = k_pages[p][...].astype(BF16)
        vb[p * PAGE_SIZE:(p + 1) * PAGE_SIZE, :] = v_pages[p][...].astype(BF16)
    z = _qk(q_ref[0], kb[...])
    z_st = jnp.concatenate([z[:, p * PAGE_SIZE:(p + 1) * PAGE_SIZE] for p in range(npages)], axis=0)
    hi, lo = _split_bf16(_softplus(z_st))
    tri = tri_ref[...]
    sit = jnp.dot(hi, tri, preferred_element_type=F32) + jnp.dot(lo, tri, preferred_element_type=F32)
    si, tot = sit[:, :PAGE_SIZE], sit[:, PAGE_SIZE:]
    run = c_scr[...]
    later = [None] * npages
    for p in range(npages - 1, -1, -1):
        later[p] = run
        run = run + tot[p * r:(p + 1) * r]
    c_scr[...] = run
    w_st = jnp.exp(z_st - si - jnp.concatenate(later, axis=0))
    w = jnp.concatenate([w_st[p * r:(p + 1) * r] for p in range(npages)], axis=1)
    acc_scr[...] += jnp.dot(w.astype(BF16), vb[...], preferred_element_type=F32)

    @pl.when(c == pl.num_programs(1) - 1)
    def _():
        o_ref[0] = acc_scr[...]


def _page_specs(npages, width, reverse, nchunks):
    specs = []
    for p in range(npages):
        if reverse:
            imap = lambda b, c, pt, p=p: (pt[b, (nchunks - 1 - c) * npages + p], 0, 0)
        else:
            imap = lambda b, c, pt, p=p: (pt[b, c * npages + p], 0, 0)
        specs.append(pl.BlockSpec((None, PAGE_SIZE, width), imap))
    return specs


def _decode_attention(page_table, q_parts, k_pools, v_pool, mode, k_new=None, v_new=None, name="decode"):
    b, n_pages = page_table.shape
    npages = math.gcd(DECODE_PAGES_PER_STEP, n_pages)
    nchunks = n_pages // npages
    r = q_parts[0].shape[1]
    shared_v = v_pool is None
    dv = k_pools[0].shape[-1] if shared_v else v_pool.shape[-1]
    rowspec = lambda a: pl.BlockSpec((1,) + a.shape[1:], lambda i, c, pt: (i, 0, 0))
    tk = npages * PAGE_SIZE
    if mode == "sb":
        tri = jnp.concatenate([_suffix_ones(PAGE_SIZE), jnp.ones((PAGE_SIZE, PAGE_SIZE), BF16)], axis=1)
        args = [q_parts[0]] + [k_pools[0]] * npages + [v_pool] * npages + [tri]
        in_specs = ([rowspec(q_parts[0])] + _page_specs(npages, k_pools[0].shape[-1], True, nchunks)
                    + _page_specs(npages, dv, True, nchunks)
                    + [pl.BlockSpec(tri.shape, lambda i, c, pt: (0, 0))])
        scratch = [pltpu.VMEM((tk, k_pools[0].shape[-1]), BF16), pltpu.VMEM((tk, dv), BF16),
                   pltpu.VMEM((r, LANES), F32), pltpu.VMEM((r, dv), F32)]
        kern = functools.partial(_decode_sb_kernel, npages=npages)
    else:
        nparts = len(q_parts)
        args = list(q_parts) + list(k_new) + ([] if shared_v else [v_new])
        in_specs = [rowspec(a) for a in args]
        for kp in k_pools:
            args += [kp] * npages
            in_specs += _page_specs(npages, kp.shape[-1], False, nchunks)
        if not shared_v:
            args += [v_pool] * npages
            in_specs += _page_specs(npages, dv, False, nchunks)
        scratch = [pltpu.VMEM((tk, kp.shape[-1]), BF16) for kp in k_pools]
        if not shared_v:
            scratch.append(pltpu.VMEM((tk, dv), BF16))
        scratch += [pltpu.VMEM((r, 1), F32), pltpu.VMEM((r, 1), F32), pltpu.VMEM((r, dv), F32)]
        kern = functools.partial(_decode_softmax_kernel, npages=npages, nparts=nparts, shared_v=shared_v)
    grid_spec = pltpu.PrefetchScalarGridSpec(
        num_scalar_prefetch=1, grid=(b, nchunks), in_specs=in_specs,
        out_specs=pl.BlockSpec((1, r, dv), lambda i, c, pt: (i, 0, 0)), scratch_shapes=scratch)
    return pl.pallas_call(kern, out_shape=jax.ShapeDtypeStruct((b, r, dv), F32), grid_spec=grid_spec,
                          compiler_params=_cparams(("arbitrary", "arbitrary")), name=name)(page_table, *args)


def _headwise_kernel(x_ref, w_ref, o_ref):
    h, din, dout = w_ref.shape
    for i in range(h):
        o_ref[:, i * dout:(i + 1) * dout] = jnp.dot(x_ref[:, i * din:(i + 1) * din].astype(BF16), w_ref[i],
                                                    preferred_element_type=F32)


def _headwise(x, w, name):
    n = x.shape[0]
    h, din, dout = w.shape
    return pl.pallas_call(_headwise_kernel, out_shape=jax.ShapeDtypeStruct((n, h * dout), F32),
                          name=name)(x, w.astype(BF16))


def _moe_kernel(x_ref, g_ref, wrh_ref, wrl_ref, br_ref, wg_ref, wu_ref, wd_ref, fg_ref, o_ref,
                xn_scr, comb_scr, acc_scr, *, final_norm):
    e = pl.program_id(1)
    tm = x_ref.shape[0]
    lane = lax.broadcasted_iota(jnp.int32, (tm, LANES), 1)

    @pl.when(e == 0)
    def _():
        x = x_ref[...]
        xn = x * lax.rsqrt(jnp.mean(x * x, axis=-1, keepdims=True) + NORM_EPS) * g_ref[...]
        xh, xl = _split_bf16(xn)
        xn_scr[...] = xh
        wrh = wrh_ref[...]
        lg = (jnp.dot(xh, wrh, preferred_element_type=F32) + jnp.dot(xl, wrh, preferred_element_type=F32)
              + jnp.dot(xh, wrl_ref[...], preferred_element_type=F32) + br_ref[...])
        big = jnp.int32(LANES)
        is_g = jnp.logical_and(lane >= N_EXPERTS, lane < N_EXPERTS + N_GROUPS)
        gl = jnp.where(is_g, lg, -jnp.inf)
        gmax = jnp.max(gl, axis=1, keepdims=True)
        gw = 1.0 / jnp.sum(jnp.exp(gl - gmax), axis=1, keepdims=True)
        gsel = jnp.min(jnp.where(gl == gmax, lane, big), axis=1, keepdims=True) - N_EXPERTS
        in_grp = jnp.logical_and(lane < N_EXPERTS, (lane >> 3) == gsel)
        el = jnp.where(in_grp, lg, -jnp.inf)
        v1 = jnp.max(el, axis=1, keepdims=True)
        i1 = jnp.min(jnp.where(el == v1, lane, big), axis=1, keepdims=True)
        el2 = jnp.where(lane == i1, -jnp.inf, el)
        v2 = jnp.max(el2, axis=1, keepdims=True)
        i2 = jnp.min(jnp.where(el2 == v2, lane, big), axis=1, keepdims=True)
        e2 = jnp.exp(v2 - v1)
        w1 = gw / (1.0 + e2)
        comb_scr[...] = jnp.where(lane == i1, w1, 0.0) + jnp.where(lane == i2, w1 * e2, 0.0)
        acc_scr[...] = jnp.zeros(acc_scr.shape, F32)

    xb = xn_scr[...]
    h = jnp.dot(xb, wg_ref[0], preferred_element_type=F32)
    u = jnp.dot(xb, wu_ref[0], preferred_element_type=F32)
    ce = jnp.sum(jnp.where(lane == e, comb_scr[...], 0.0), axis=1, keepdims=True)
    act = h * (1.0 / (1.0 + jnp.exp(-h))) * u * ce
    acc_scr[...] += jnp.dot(act.astype(BF16), wd_ref[0], preferred_element_type=F32)

    @pl.when(e == pl.num_programs(1) - 1)
    def _():
        y = x_ref[...] + acc_scr[...]
        if final_norm:
            y = y * lax.rsqrt(jnp.mean(y * y, axis=-1, keepdims=True) + NORM_EPS) * fg_ref[...]
        o_ref[...] = y


def _moe(x, norm_g, w_group, b_group, w_router, b_router, w_gate, w_up, w_down, final_g, name):
    n, d = x.shape
    ne, _, f = w_gate.shape
    tm = _row_tile(n, 1024)
    wr = jnp.zeros((d, LANES), F32)
    wr = wr.at[:, :N_EXPERTS].set(w_router.reshape(d, N_EXPERTS)).at[:, N_EXPERTS:N_EXPERTS + N_GROUPS].set(w_group)
    br = jnp.zeros((1, LANES), F32)
    br = br.at[0, :N_EXPERTS].set(b_router.reshape(N_EXPERTS)).at[0, N_EXPERTS:N_EXPERTS + N_GROUPS].set(b_group)
    wrh = wr.astype(BF16)
    wrl = (wr - wrh.astype(F32)).astype(BF16)
    fg = jnp.ones((1, d), F32) if final_g is None else final_g.reshape(1, d).astype(F32)
    row = lambda i, e: (i, 0)
    const = lambda i, e: (0, 0)
    in_specs = [pl.BlockSpec((tm, d), row), pl.BlockSpec((1, d), const), pl.BlockSpec((d, LANES), const),
                pl.BlockSpec((d, LANES), const), pl.BlockSpec((1, LANES), const),
                pl.BlockSpec((1, d, f), lambda i, e: (e, 0, 0)), pl.BlockSpec((1, d, f), lambda i, e: (e, 0, 0)),
                pl.BlockSpec((1, f, d), lambda i, e: (e, 0, 0)), pl.BlockSpec((1, d), const)]
    return pl.pallas_call(
        functools.partial(_moe_kernel, final_norm=final_g is not None),
        out_shape=jax.ShapeDtypeStruct((n, d), F32), grid=(n // tm, ne), in_specs=in_specs,
        out_specs=pl.BlockSpec((tm, d), row),
        scratch_shapes=[pltpu.VMEM((tm, d), BF16), pltpu.VMEM((tm, LANES), F32), pltpu.VMEM((tm, d), F32)],
        compiler_params=_cparams(("arbitrary", "arbitrary")), name=name,
    )(x, norm_g.reshape(1, d).astype(F32), wrh, wrl, br, w_gate.astype(BF16), w_up.astype(BF16),
      w_down.astype(BF16), fg)


def _rwkv_slot(s, idx, lane, r_ref, w_ref, k_ref, kk_ref, kka_ref, vt_ref, y_scr, c):
    row = lambda ref: ref[c, pl.ds(idx, 1), :]
    sel = lane == idx
    vcol = jnp.sum(jnp.where(sel, vt_ref[c], 0.0), axis=1, keepdims=True)
    sa = -jnp.sum(s * row(kk_ref), axis=1, keepdims=True)
    s = s * row(w_ref) + sa * row(kka_ref) + vcol * row(k_ref)
    y = jnp.sum(s * row(r_ref), axis=1, keepdims=True)
    y_scr[c] = jnp.where(sel, y, y_scr[c])
    return s


def _rwkv_scan_kernel(r_ref, w_ref, k_ref, kk_ref, kka_ref, vt_ref, yt_ref, sfin_ref, s_scr, y_scr):
    tblk = pl.program_id(0)
    nch, tb, hs = r_ref.shape

    @pl.when(tblk == 0)
    def _():
        s_scr[...] = jnp.zeros(s_scr.shape, F32)

    lane = lax.broadcasted_iota(jnp.int32, (hs, tb), 1)
    y_scr[...] = jnp.zeros(y_scr.shape, F32)

    def step(t, carry):
        for c in range(nch):
            s_scr[c] = _rwkv_slot(s_scr[c], t, lane, r_ref, w_ref, k_ref, kk_ref, kka_ref, vt_ref, y_scr, c)
        return carry

    lax.fori_loop(0, tb, step, 0)
    yt_ref[...] = y_scr[...]

    @pl.when(tblk == pl.num_programs(0) - 1)
    def _():
        sfin_ref[...] = s_scr[...]


def _rwkv_scan(r, w, k, kk, kka, vt):
    nch, t, hs = r.shape
    tb = min(t, LANES)
    assert t % tb == 0
    rows = pl.BlockSpec((nch, tb, hs), lambda i: (0, i, 0))
    cols = pl.BlockSpec((nch, hs, tb), lambda i: (0, 0, i))
    return pl.pallas_call(
        _rwkv_scan_kernel,
        out_shape=[jax.ShapeDtypeStruct((nch, hs, t), F32), jax.ShapeDtypeStruct((nch, hs, hs), F32)],
        grid=(t // tb,), in_specs=[rows] * 5 + [cols],
        out_specs=[cols, pl.BlockSpec((nch, hs, hs), lambda i: (0, 0, 0))],
        scratch_shapes=[pltpu.VMEM((nch, hs, hs), F32), pltpu.VMEM((nch, hs, tb), F32)],
        compiler_params=_cparams(("arbitrary",)), name="rwkv_scan")(r, w, k, kk, kka, vt)


def _rwkv_step_kernel(s_ref, r_ref, w_ref, k_ref, kk_ref, kka_ref, vt_ref, so_ref, yt_ref, y_scr):
    nb, hs = r_ref.shape[1], r_ref.shape[2]
    lane = lax.broadcasted_iota(jnp.int32, (hs, nb), 1)
    y_scr[...] = jnp.zeros(y_scr.shape, F32)

    def step(b, carry):
        so_ref[b, 0] = _rwkv_slot(s_ref[b, 0], b, lane, r_ref, w_ref, k_ref, kk_ref, kka_ref, vt_ref, y_scr, 0)
        return carry

    lax.fori_loop(0, nb, step, 0)
    yt_ref[...] = y_scr[...]


def _rwkv_step(s, r, w, k, kk, kka, vt):
    b, h, hs, _ = s.shape
    nb = min(b, LANES)
    assert b % nb == 0
    sspec = pl.BlockSpec((nb, 1, hs, hs), lambda j, i: (i, j, 0, 0))
    rows = pl.BlockSpec((1, nb, hs), lambda j, i: (j, i, 0))
    cols = pl.BlockSpec((1, hs, nb), lambda j, i: (j, 0, i))
    return pl.pallas_call(
        _rwkv_step_kernel,
        out_shape=[jax.ShapeDtypeStruct(s.shape, F32), jax.ShapeDtypeStruct((h, hs, b), F32)],
        grid=(h, b // nb), in_specs=[sspec] + [rows] * 5 + [cols], out_specs=[sspec, cols],
        scratch_shapes=[pltpu.VMEM((1, hs, nb), F32)],
        compiler_params=_cparams(("arbitrary", "arbitrary")), name="rwkv_step")(s, r, w, k, kk, kka, vt)


def _rope_tables(pos, reps):
    half = MLA_ROPE // 2
    inv = ROPE_BASE ** (-jnp.arange(half, dtype=F32) / half)
    ang = pos.astype(F32)[:, None] * inv[None, :]
    cos, sin = jnp.cos(ang), jnp.sin(ang)
    return jnp.tile(jnp.concatenate([cos, cos], -1), (1, reps)), jnp.tile(jnp.concatenate([-sin, sin], -1), (1, reps))


def _swap_halves(w, width):
    k, m = w.shape
    w = w.reshape(k, m // width, 2, width // 2)
    return jnp.concatenate([w[:, :, 1], w[:, :, 0]], axis=2).reshape(k, m)


def _even_weights(w_in, q_norm, w_uq, kv_norm, w_uk, w_uv, w_out):
    kpe_lo = MLA_Q_LORA + MLA_KV_LORA
    w_in_ext = jnp.concatenate([w_in, _swap_halves(w_in[:, kpe_lo:kpe_lo + MLA_ROPE], MLA_ROPE)], axis=1)
    wq = w_uq.reshape(MLA_Q_LORA, MLA_HEADS, MLA_NOPE + MLA_ROPE)
    wq_n = wq[:, :, :MLA_NOPE].reshape(MLA_Q_LORA, -1)
    wq_p = wq[:, :, MLA_NOPE:].reshape(MLA_Q_LORA, -1)
    wq_ext = jnp.concatenate([wq_n, wq_p, _swap_halves(wq_p, MLA_ROPE)], axis=1)
    w_kv = jnp.concatenate([w_uk.reshape(MLA_KV_LORA, -1), w_uv.reshape(MLA_KV_LORA, -1)], axis=1)
    return dict(w_in=w_in_ext, q_norm=q_norm, wq=wq_ext, kv_norm=kv_norm, w_kv=w_kv, w_uk=w_uk, w_uv=w_uv,
                w_out=w_out)


def _even_project(h_in, attn_g, wts, pos, b, t):
    n = h_in.shape[0]
    proj = _linear(h_in, wts["w_in"], g=attn_g, name="even_in")
    o = 0
    parts = []
    for wdt in (MLA_Q_LORA, MLA_KV_LORA, MLA_ROPE, SB_HEADS * SB_HD, SB_KV_HEADS * SB_HD, SB_KV_HEADS * SB_HD,
                MLA_ROPE):
        parts.append(proj[:, o:o + wdt])
        o += wdt
    q_lat, kv_lat, kpe_raw, sq, sk, sv, kpe_sw = parts
    cos1, sin1 = _rope_tables(pos, 1)
    cos1, sin1 = jnp.tile(cos1, (b, 1)), jnp.tile(sin1, (b, 1))
    kpe = kpe_raw * cos1 + kpe_sw * sin1
    qall = _linear(q_lat, wts["wq"], g=wts["q_norm"], name="mla_q_up")
    hp = MLA_HEADS * MLA_ROPE
    q_nope, q_pe, q_pe_sw = qall[:, :MLA_HEADS * MLA_NOPE], qall[:, -2 * hp:-hp], qall[:, -hp:]
    q_pe = q_pe * jnp.tile(cos1, (1, MLA_HEADS)) + q_pe_sw * jnp.tile(sin1, (1, MLA_HEADS))
    return q_nope, q_pe, kv_lat, kpe, sq, sk, sv


def _heads_first(x, b, t, h):
    d = x.shape[1] // h
    return x.reshape(b, t, h, d).transpose(0, 2, 1, 3).reshape(b * h, t, d)


def _even_prompt(x, attn_g, wts, b, t):
    pos = jnp.arange(t, dtype=jnp.int32)
    q_nope, q_pe, kv_lat, kpe, sq, sk, sv = _even_project(x, attn_g, wts, pos, b, t)
    kexp, ckv = _linear(kv_lat, wts["w_kv"], g=wts["kv_norm"], emit_normed=True, name="mla_kv_up")
    hn = MLA_HEADS * MLA_NOPE
    scale = (MLA_NOPE + MLA_ROPE) ** -0.5
    q = jnp.concatenate([_heads_first(q_nope, b, t, MLA_HEADS), _heads_first(q_pe, b, t, MLA_HEADS)], axis=-1)
    q = (q * scale).astype(BF16)[:, None]
    k = jnp.concatenate([_heads_first(kexp[:, :hn], b, t, MLA_HEADS),
                         jnp.broadcast_to(kpe.reshape(b, 1, t, MLA_ROPE), (b, MLA_HEADS, t, MLA_ROPE)
                                          ).reshape(b * MLA_HEADS, t, MLA_ROPE)], axis=-1).astype(BF16)
    v = _heads_first(kexp[:, hn:], b, t, MLA_HEADS).astype(BF16)
    mla = _causal_attention(q, k, v, "softmax", 512, "mla_prompt")
    mla = mla.reshape(b, MLA_HEADS, t, MLA_V).transpose(0, 2, 1, 3).reshape(b * t, MLA_HEADS * MLA_V)
    grp = SB_HEADS // SB_KV_HEADS
    sq_h = (sq * SB_HD ** -0.5).astype(BF16).reshape(b, t, SB_KV_HEADS, grp, SB_HD).transpose(0, 2, 3, 1, 4)
    sq_h = sq_h.reshape(b * SB_KV_HEADS, grp, t, SB_HD)
    sbo = _causal_attention(sq_h, _heads_first(sk, b, t, SB_KV_HEADS).astype(BF16),
                            _heads_first(sv, b, t, SB_KV_HEADS).astype(BF16), "sb", 256, "sb_prompt")
    sbo = sbo.reshape(b, SB_KV_HEADS, grp, t, SB_HD).transpose(0, 3, 1, 2, 4).reshape(b * t, SB_HEADS * SB_HD)
    y = _outproj(mla, sbo, wts["w_out"], x, name="even_out")
    return y, (ckv, kpe, sk, sv)


def _even_sample(x, attn_g, wts, page_table, c_ckv, c_kpe, c_sk, c_sv):
    b = x.shape[0]
    past_len = page_table.shape[1] * PAGE_SIZE
    pos = jnp.full((1,), past_len, jnp.int32)
    q_nope, q_pe, kv_lat, kpe, sq, sk, sv = _even_project(x, attn_g, wts, pos, b, 1)
    _, ckv = _linear(kv_lat, wts["w_kv"], g=wts["kv_norm"], emit_normed=True, name="mla_kv_norm_s")
    scale = (MLA_NOPE + MLA_ROPE) ** -0.5
    q_abs = _headwise(q_nope, wts["w_uk"].transpose(1, 2, 0), "mla_q_abs")
    q_abs = (q_abs * scale).astype(BF16).reshape(b, MLA_HEADS, MLA_KV_LORA)
    q_pe = (q_pe * scale).astype(BF16).reshape(b, MLA_HEADS, MLA_ROPE)
    lat = _decode_attention(page_table, [q_abs, q_pe], [c_ckv, c_kpe], None, "softmax",
                            k_new=[ckv[:, None], kpe[:, None]], name="mla_decode")
    mla = _headwise(lat.reshape(b, MLA_HEADS * MLA_KV_LORA), wts["w_uv"].transpose(1, 0, 2), "mla_v_up")
    npool = c_sk.shape[0]
    kvw = SB_KV_HEADS * SB_HD
    grp = SB_HEADS // SB_KV_HEADS
    sq_rows = (sq * SB_HD ** -0.5).reshape(b, SB_KV_HEADS, grp, 1, SB_HD) * jnp.eye(SB_KV_HEADS, dtype=F32)[
        None, :, None, :, None]
    sq_rows = sq_rows.reshape(b, SB_HEADS, kvw).astype(BF16)
    sbo = _decode_attention(page_table, [sq_rows], [c_sk.reshape(npool, PAGE_SIZE, kvw)],
                            c_sv.reshape(npool, PAGE_SIZE, kvw), "sb", name="sb_decode")
    sbo = sbo.reshape(b, SB_KV_HEADS, grp, SB_KV_HEADS, SB_HD)
    sbo = jnp.einsum("bkgjd,kj->bkgd", sbo, jnp.eye(SB_KV_HEADS, dtype=F32)).reshape(b, SB_HEADS * SB_HD)
    y = _outproj(mla, sbo, wts["w_out"], x, name="even_out_s")
    return y, (ckv, kpe, sk, sv)


def _rwkv_prepare(rw, prev, p):
    xs = rw + (prev - rw) * p["mu"]
    o = 0
    parts = []
    for wdt in (RW_DIM, RW_DIM, RW_DIM, RW_W_LORA, RW_A_LORA, RW_G_LORA):
        parts.append(xs[:, o:o + wdt])
        o += wdt
    r, k, v, xw, xa, xg = parts
    w_log = -jax.nn.softplus(-(p["w0"] + _linear(jnp.tanh(xw), p["w2"], name="rw_w_lora"))) - 0.5
    decay = jnp.exp(-jnp.exp(w_log))
    a = jax.nn.sigmoid(p["a0"] + _linear(xa, p["a2"], name="rw_a_lora"))
    g = _linear(jax.nn.sigmoid(xg), p["g2"], name="rw_g_lora")
    n = r.shape[0]
    kk = (k * p["kk"]).reshape(n, RW_HEADS, RW_HS)
    kk = (kk * lax.rsqrt(jnp.maximum(jnp.sum(kk * kk, axis=-1, keepdims=True), 1e-24))).reshape(n, RW_DIM)
    k = k * (1.0 + (a - 1.0) * p["ka"])
    return r, decay, k, v, kk, kk * a, g


def _rwkv_finish(yr, r, k, v, g, p):
    n = yr.shape[0]
    yh = yr.reshape(n, RW_HEADS, RW_HS)
    mean = jnp.mean(yh, axis=-1, keepdims=True)
    var = jnp.mean(jnp.square(yh - mean), axis=-1, keepdims=True)
    yn = ((yh - mean) * lax.rsqrt(var + RW_LN_EPS)).reshape(n, RW_DIM) * p["ln_w"] + p["ln_b"]
    rk = (r * k * p["rk"].reshape(1, RW_DIM)).reshape(n, RW_HEADS, RW_HS)
    bonus = (jnp.sum(rk, axis=-1, keepdims=True) * v.reshape(n, RW_HEADS, RW_HS)).reshape(n, RW_DIM)
    return (yn + bonus) * g


def _odd_split(proj):
    dq_w = DF_HEADS * 2 * DF_HD
    dkv_w = DF_KV_HEADS * 2 * DF_HD
    return proj[:, :dq_w], proj[:, dq_w:dq_w + dkv_w], proj[:, dq_w + dkv_w:dq_w + 2 * dkv_w], proj[:, dq_w + 2 * dkv_w:]


def _odd_prompt(x, attn_g, p, lam, lam_init, b, t):
    n = x.shape[0]
    proj = _linear(x, p["w_in"], g=attn_g, name="odd_in")
    dq, dk, dv, rw = _odd_split(proj)
    grp = DF_HEADS // DF_KV_HEADS
    q = (dq * DF_HD ** -0.5).astype(BF16).reshape(b, t, DF_KV_HEADS, grp, 2, DF_HD).transpose(0, 2, 4, 3, 1, 5)
    q = q.reshape(b * DF_KV_HEADS * 2, grp, t, DF_HD)
    k = dk.astype(BF16).reshape(b, t, DF_KV_HEADS * 2, DF_HD).transpose(0, 2, 1, 3).reshape(-1, t, DF_HD)
    v = _heads_first(dv, b, t, DF_KV_HEADS).astype(BF16)
    o = _causal_attention(q, k, v, "softmax", 512, "diff_prompt")
    o = o.reshape(b, DF_KV_HEADS, 2, grp, t, 2 * DF_HD).transpose(2, 0, 4, 1, 3, 5).reshape(2, n, DF_HEADS * 2 * DF_HD)
    rwb = rw.reshape(b, t, -1)
    prev = jnp.concatenate([jnp.zeros((b, 1, rwb.shape[-1]), F32), rwb[:, :-1]], axis=1).reshape(n, -1)
    r, decay, k_, v_, kk, kka, g = _rwkv_prepare(rw, prev, p)
    hf = lambda z: _heads_first(z, b, t, RW_HEADS)
    vt = hf(v_).transpose(0, 2, 1)
    yt, s_fin = _rwkv_scan(hf(r), hf(decay), hf(k_), hf(kk), hf(kka), vt)
    yr = yt.reshape(b, RW_HEADS, RW_HS, t).transpose(0, 3, 1, 2).reshape(n, RW_DIM)
    rw_out = _rwkv_finish(yr, r, k_, v_, g, p)
    y = _outproj((o[0], o[1]), rw_out, p["w_out"], x, diff=(lam, p["subln"], 1.0 - lam_init, DF_HEADS),
                 name="odd_out")
    s_fin = s_fin.reshape(b, RW_HEADS, RW_HS, RW_HS)
    return y, (dk, dv), s_fin, rwb[:, -1]


def _odd_sample(x, attn_g, p, lam, lam_init, page_table, c_dk, c_dv, shift_prev, s_prev):
    b = x.shape[0]
    proj = _linear(x, p["w_in"], g=attn_g, name="odd_in_s")
    dq, dk, dv, rw = _odd_split(proj)
    npool = c_dk.shape[0]
    kw = DF_KV_HEADS * 2 * DF_HD
    q_rows = _block_rows_kgc(dq * DF_HD ** -0.5).astype(BF16)
    o = _decode_attention(page_table, [q_rows], [c_dk.reshape(npool, PAGE_SIZE, kw)],
                          c_dv.reshape(npool, PAGE_SIZE, kw), "softmax",
                          k_new=[dk[:, None]], v_new=dv[:, None], name="diff_decode")
    grp = DF_HEADS // DF_KV_HEADS
    o = o.reshape(b, DF_KV_HEADS, grp, 2, DF_KV_HEADS, 2 * DF_HD)
    o = jnp.einsum("bkgcje,kj->cbkge", o, jnp.eye(DF_KV_HEADS, dtype=F32)).reshape(2, b, DF_HEADS * 2 * DF_HD)
    r, decay, k_, v_, kk, kka, g = _rwkv_prepare(rw, shift_prev, p)
    hb = lambda z: z.reshape(b, RW_HEADS, RW_HS).transpose(1, 0, 2)
    s_new, yt = _rwkv_step(s_prev, hb(r), hb(decay), hb(k_), hb(kk), hb(kka), hb(v_).transpose(0, 2, 1))
    yr = yt.transpose(2, 0, 1).reshape(b, RW_DIM)
    rw_out = _rwkv_finish(yr, r, k_, v_, g, p)
    y = _outproj((o[0], o[1]), rw_out, p["w_out"], x, diff=(lam, p["subln"], 1.0 - lam_init, DF_HEADS),
                 name="odd_out_s")
    return y, (dk, dv), s_new, rw


def _block_rows_kgc(dq):
    b = dq.shape[0]
    grp = DF_HEADS // DF_KV_HEADS
    q = dq.reshape(b, DF_KV_HEADS, grp, 2, 1, 1, DF_HD)
    eye_k = jnp.eye(DF_KV_HEADS, dtype=dq.dtype)[None, :, None, None, :, None, None]
    eye_c = jnp.eye(2, dtype=dq.dtype)[None, None, None, :, None, :, None]
    return (q * eye_k * eye_c).reshape(b, DF_KV_HEADS * grp * 2, DF_KV_HEADS * 2 * DF_HD)


def kernel(x_prompt, x_sample, cache_mla_ckv, cache_mla_kpe, cache_sb_k, cache_sb_v, cache_df_k, cache_df_v, state_rw_s, state_rw_shift, page_table, attn_norm, ffn_norm, final_norm, ev_w_in, mla_q_norm, mla_w_uq, mla_kv_norm, mla_w_uk, mla_w_uv, ev_w_out, od_w_in, df_lq1, df_lk1, df_lq2, df_lk2, df_subln, rw_mu, rw_w0, rw_w2, rw_a0, rw_a2, rw_g2, rw_kk, rw_ka, rw_rk, rw_ln_w, rw_ln_b, od_w_out, moe_w_group, moe_b_group, moe_w_router, moe_b_router, moe_w_gate, moe_w_up, moe_w_down):
    bp, tp, d = x_prompt.shape
    bs, ts, _ = x_sample.shape
    assert ts == 1
    depth = attn_norm.shape[0]
    xp = x_prompt.reshape(bp * tp, d)
    xs = x_sample.reshape(bs, d)
    outs = {k: [] for k in ("ckv_p", "ckv_s", "kpe_p", "kpe_s", "sbk_p", "sbk_s", "sbv_p", "sbv_s", "dfk_p", "dfk_s",
                            "dfv_p", "dfv_s", "rws_p", "rws_s", "rwsh_p", "rwsh_s")}
    for layer in range(depth):
        i = layer // 2
        if layer % 2 == 0:
            wts = _even_weights(ev_w_in[i], mla_q_norm[i], mla_w_uq[i], mla_kv_norm[i], mla_w_uk[i], mla_w_uv[i],
                                ev_w_out[i])
            xp, rp = _even_prompt(xp, attn_norm[layer], wts, bp, tp)
            xs, rs = _even_sample(xs, attn_norm[layer], wts, page_table, cache_mla_ckv[i], cache_mla_kpe[i],
                                  cache_sb_k[i], cache_sb_v[i])
            outs["ckv_p"].append(rp[0].reshape(bp, tp, MLA_KV_LORA))
            outs["kpe_p"].append(rp[1].reshape(bp, tp, MLA_ROPE))
            outs["sbk_p"].append(rp[2].reshape(bp, tp, SB_KV_HEADS, SB_HD))
            outs["sbv_p"].append(rp[3].reshape(bp, tp, SB_KV_HEADS, SB_HD))
            outs["ckv_s"].append(rs[0].reshape(bs, ts, MLA_KV_LORA))
            outs["kpe_s"].append(rs[1].reshape(bs, ts, MLA_ROPE))
            outs["sbk_s"].append(rs[2].reshape(bs, ts, SB_KV_HEADS, SB_HD))
            outs["sbv_s"].append(rs[3].reshape(bs, ts, SB_KV_HEADS, SB_HD))
        else:
            lam_init = 0.8 - 0.6 * math.exp(-0.3 * layer)
            lam = (jnp.exp(jnp.sum(df_lq1[i] * df_lk1[i])) - jnp.exp(jnp.sum(df_lq2[i] * df_lk2[i])) + lam_init)
            p = dict(w_in=od_w_in[i], subln=df_subln[i], mu=rw_mu[i], w0=rw_w0[i], w2=rw_w2[i], a0=rw_a0[i],
                     a2=rw_a2[i], g2=rw_g2[i], kk=rw_kk[i], ka=rw_ka[i], rk=rw_rk[i], ln_w=rw_ln_w[i],
                     ln_b=rw_ln_b[i], w_out=od_w_out[i])
            xp, kvp, sp, shp = _odd_prompt(xp, attn_norm[layer], p, lam, lam_init, bp, tp)
            xs, kvs, ss, shs = _odd_sample(xs, attn_norm[layer], p, lam, lam_init, page_table, cache_df_k[i],
                                           cache_df_v[i], state_rw_shift[i], state_rw_s[i])
            outs["dfk_p"].append(kvp[0].reshape(bp, tp, DF_KV_HEADS, 2, DF_HD))
            outs["dfv_p"].append(kvp[1].reshape(bp, tp, DF_KV_HEADS, 2 * DF_HD))
            outs["dfk_s"].append(kvs[0].reshape(bs, ts, DF_KV_HEADS, 2, DF_HD))
            outs["dfv_s"].append(kvs[1].reshape(bs, ts, DF_KV_HEADS, 2 * DF_HD))
            outs["rws_p"].append(sp)
            outs["rws_s"].append(ss)
            outs["rwsh_p"].append(shp)
            outs["rwsh_s"].append(shs)
        fin = final_norm if layer == depth - 1 else None
        mw = (moe_w_group[layer], moe_b_group[layer], moe_w_router[layer], moe_b_router[layer], moe_w_gate[layer],
              moe_w_up[layer], moe_w_down[layer])
        xp = _moe(xp, ffn_norm[layer], *mw, fin, "moe_p")
        xs = _moe(xs, ffn_norm[layer], *mw, fin, "moe_s")
    st = lambda key: jnp.stack(outs[key], axis=0)
    return (xp.reshape(bp, tp, d), xs.reshape(bs, ts, d), st("ckv_p"), st("ckv_s"), st("kpe_p"), st("kpe_s"),
            st("sbk_p"), st("sbk_s"), st("sbv_p"), st("sbv_s"), st("dfk_p"), st("dfk_s"), st("dfv_p"), st("dfv_s"),
            st("rws_p"), st("rws_s"), st("rwsh_p"), st("rwsh_s"))
```

```python
import functools
import math

import jax
import jax.numpy as jnp
from jax import lax
from jax.experimental import pallas as pl
from jax.experimental.pallas import tpu as pltpu

F32 = jnp.float32
BF16 = jnp.bfloat16

PAGE_SIZE = 128
NORM_EPS = 1e-6
MLA_HEADS, MLA_Q_LORA, MLA_KV_LORA, MLA_NOPE, MLA_ROPE, MLA_V = 8, 384, 256, 64, 32, 64
ROPE_BASE = 10000.0
SB_HEADS, SB_KV_HEADS, SB_HD = 8, 2, 64
DF_HEADS, DF_KV_HEADS, DF_HD = 4, 2, 64
RW_HEADS, RW_HS = 8, 64
RW_DIM = RW_HEADS * RW_HS
RW_W_LORA, RW_A_LORA, RW_G_LORA = 64, 64, 128
RW_LN_EPS = 64e-5
N_GROUPS, EXPERTS_PER_GROUP = 4, 8
N_EXPERTS = N_GROUPS * EXPERTS_PER_GROUP
LANES = 128
VMEM_LIMIT = 48 * 1024 * 1024
DECODE_PAGES_PER_STEP = 32


def _cparams(sem):
    return pltpu.CompilerParams(dimension_semantics=sem, vmem_limit_bytes=VMEM_LIMIT)


def _row_tile(n, pref):
    t = min(n, pref)
    while n % t:
        t //= 2
    return t


def _linear_kernel(*refs, norm, has_res, emit_normed):
    it = iter(refs)
    x_ref = next(it)
    g_ref = next(it) if norm else None
    w_ref = next(it)
    res_ref = next(it) if has_res else None
    o_ref = next(it)
    xn_ref = next(it) if emit_normed else None
    x = x_ref[...].astype(F32)
    if norm:
        x = x * lax.rsqrt(jnp.mean(x * x, axis=-1, keepdims=True) + NORM_EPS) * g_ref[...]
        if emit_normed:
            xn_ref[...] = x
    y = jnp.dot(x.astype(BF16), w_ref[...], preferred_element_type=F32)
    if has_res:
        y = y + res_ref[...]
    o_ref[...] = y.astype(o_ref.dtype)


def _linear(x, w, g=None, res=None, emit_normed=False, name="linear"):
    n, k = x.shape
    m = w.shape[1]
    tm = _row_tile(n, 512)
    tn = m if (m <= 2048 or emit_normed) else m // 2
    assert m % tn == 0 and (tn == m or tn % LANES == 0)
    grid = (m // tn, n // tm)
    in_specs = [pl.BlockSpec((tm, k), lambda j, i: (i, 0))]
    args = [x]
    if g is not None:
        in_specs.append(pl.BlockSpec((1, k), lambda j, i: (0, 0)))
        args.append(g.reshape(1, k).astype(F32))
    in_specs.append(pl.BlockSpec((k, tn), lambda j, i: (0, j)))
    args.append(w.astype(BF16))
    if res is not None:
        in_specs.append(pl.BlockSpec((tm, tn), lambda j, i: (i, j)))
        args.append(res)
    out_shape = [jax.ShapeDtypeStruct((n, m), F32)]
    out_specs = [pl.BlockSpec((tm, tn), lambda j, i: (i, j))]
    if emit_normed:
        out_shape.append(jax.ShapeDtypeStruct((n, k), F32))
        out_specs.append(pl.BlockSpec((tm, k), lambda j, i: (i, 0)))
    outs = pl.pallas_call(
        functools.partial(_linear_kernel, norm=g is not None, has_res=res is not None, emit_normed=emit_normed),
        out_shape=out_shape, grid=grid, in_specs=in_specs, out_specs=out_specs,
        compiler_params=_cparams(("arbitrary", "arbitrary")), name=name)(*args)
    return outs if emit_normed else outs[0]


def _outproj_kernel(*refs, diff_heads, diff_scale):
    if diff_heads:
        a0_ref, a1_ref, lam_ref, sub_ref, b_ref, wa_ref, wb_ref, x_ref, o_ref = refs
        lam = lam_ref[0, 0]
        hd = a0_ref.shape[1] // diff_heads
        y = x_ref[...] + jnp.dot(b_ref[...].astype(BF16), wb_ref[...], preferred_element_type=F32)
        for h in range(diff_heads):
            sl = slice(h * hd, (h + 1) * hd)
            o = a0_ref[:, sl] - lam * a1_ref[:, sl]
            o = o * lax.rsqrt(jnp.mean(o * o, axis=-1, keepdims=True) + NORM_EPS) * sub_ref[...] * diff_scale
            y = y + jnp.dot(o.astype(BF16), wa_ref[sl, :], preferred_element_type=F32)
    else:
        a_ref, b_ref, wa_ref, wb_ref, x_ref, o_ref = refs
        y = x_ref[...] + jnp.dot(a_ref[...].astype(BF16), wa_ref[...], preferred_element_type=F32)
        y = y + jnp.dot(b_ref[...].astype(BF16), wb_ref[...], preferred_element_type=F32)
    o_ref[...] = y


def _outproj(a, b, w, x, diff=None, name="outproj"):
    n, d = x.shape
    ka, kb = (a[0].shape[1] if diff else a.shape[1]), b.shape[1]
    tm = _row_tile(n, 512)
    row = lambda i: (i, 0)
    const = lambda i: (0, 0)
    wa, wb = w[:ka].astype(BF16), w[ka:].astype(BF16)
    if diff:
        lam, subln, scale, heads = diff
        args = [a[0], a[1], lam.reshape(1, 1).astype(F32), subln.reshape(1, -1).astype(F32), b, wa, wb, x]
        in_specs = [pl.BlockSpec((tm, ka), row), pl.BlockSpec((tm, ka), row),
                    pl.BlockSpec(memory_space=pltpu.SMEM), pl.BlockSpec((1, ka // heads), const),
                    pl.BlockSpec((tm, kb), row), pl.BlockSpec((ka, d), const), pl.BlockSpec((kb, d), const),
                    pl.BlockSpec((tm, d), row)]
        kern = functools.partial(_outproj_kernel, diff_heads=heads, diff_scale=scale)
    else:
        args = [a, b, wa, wb, x]
        in_specs = [pl.BlockSpec((tm, ka), row), pl.BlockSpec((tm, kb), row), pl.BlockSpec((ka, d), const),
                    pl.BlockSpec((kb, d), const), pl.BlockSpec((tm, d), row)]
        kern = functools.partial(_outproj_kernel, diff_heads=0, diff_scale=1.0)
    return pl.pallas_call(kern, out_shape=jax.ShapeDtypeStruct((n, d), F32), grid=(n // tm,), in_specs=in_specs,
                          out_specs=pl.BlockSpec((tm, d), row), compiler_params=_cparams(("arbitrary",)),
                          name=name)(*args)


def _qk(q, k):
    return lax.dot_general(q, k, (((1,), (1,)), ((), ())), preferred_element_type=F32)


def _softplus(z):
    return jnp.maximum(z, 0.0) + jnp.log(1.0 + jnp.exp(-jnp.abs(z)))


def _split_bf16(x):
    hi = x.astype(BF16)
    return hi, (x - hi.astype(F32)).astype(BF16)


def _lanes(x, width):
    if width <= LANES:
        return x[:, :width]
    return jnp.concatenate([x] * (width // LANES), axis=1)


def _attn_softmax_kernel(q_ref, k_ref, v_ref, o_ref, m_scr, l_scr, acc_scr, *, tq):
    i = pl.program_id(1)
    g, dv = q_ref.shape[1], v_ref.shape[-1]
    r = g * tq
    q = q_ref[0].reshape(r, q_ref.shape[-1])
    m_scr[...] = jnp.full(m_scr.shape, -jnp.inf, F32)
    l_scr[...] = jnp.zeros(l_scr.shape, F32)
    acc_scr[...] = jnp.zeros(acc_scr.shape, F32)

    def chunk(j, masked):
        start = pl.multiple_of(j * tq, tq)
        s = _qk(q, k_ref[0, pl.ds(start, tq), :])
        if masked:
            row = lax.broadcasted_iota(jnp.int32, (r, tq), 0) & (tq - 1)
            col = lax.broadcasted_iota(jnp.int32, (r, tq), 1)
            s = jnp.where(col <= row, s, -jnp.inf)
        m_prev = m_scr[...]
        m_new = jnp.maximum(m_prev, jnp.max(s, axis=1, keepdims=True))
        p = jnp.exp(s - _lanes(m_new, tq))
        alpha = jnp.exp(m_prev - m_new)
        l_scr[...] = alpha * l_scr[...] + jnp.sum(p, axis=1, keepdims=True)
        acc_scr[...] = _lanes(alpha, dv) * acc_scr[...] + jnp.dot(p.astype(BF16), v_ref[0, pl.ds(start, tq), :],
                                                                  preferred_element_type=F32)
        m_scr[...] = m_new

    def body(j, c):
        chunk(j, False)
        return c

    lax.fori_loop(0, i, body, 0)
    chunk(i, True)
    o_ref[0] = (acc_scr[...] / _lanes(l_scr[...], dv)).reshape(g, tq, dv)


def _attn_sb_kernel(q_ref, k_ref, v_ref, tri_ref, o_ref, c_scr, acc_scr, *, tq):
    i = pl.program_id(1)
    g, dv = q_ref.shape[1], v_ref.shape[-1]
    r = g * tq
    q = q_ref[0].reshape(r, q_ref.shape[-1])
    c_scr[...] = jnp.zeros(c_scr.shape, F32)
    acc_scr[...] = jnp.zeros(acc_scr.shape, F32)

    def chunk(j, masked):
        start = pl.multiple_of(j * tq, tq)
        z = _qk(q, k_ref[0, pl.ds(start, tq), :])
        sp = _softplus(z)
        if masked:
            row = lax.broadcasted_iota(jnp.int32, (r, tq), 0) & (tq - 1)
            col = lax.broadcasted_iota(jnp.int32, (r, tq), 1)
            mask = col < row
            sp = jnp.where(mask, sp, 0.0)
        hi, lo = _split_bf16(sp)
        tri = tri_ref[...]
        si = jnp.dot(hi, tri, preferred_element_type=F32) + jnp.dot(lo, tri, preferred_element_type=F32)
        c_prev = c_scr[...]
        w = jnp.exp(z - si - _lanes(c_prev, tq))
        if masked:
            w = jnp.where(mask, w, 0.0)
        acc_scr[...] += jnp.dot(w.astype(BF16), v_ref[0, pl.ds(start, tq), :], preferred_element_type=F32)
        c_scr[...] = c_prev + jnp.sum(sp, axis=1, keepdims=True)

    chunk(i, True)

    def body(jj, c):
        chunk(i - 1 - jj, False)
        return c

    lax.fori_loop(0, i, body, 0)
    o_ref[0] = acc_scr[...].reshape(g, tq, dv)


def _suffix_ones(n):
    return (lax.broadcasted_iota(jnp.int32, (n, n), 0) >= lax.broadcasted_iota(jnp.int32, (n, n), 1)).astype(BF16)


def _causal_attention(q, k, v, mode, tq, name):
    bh, g, t, dk = q.shape
    dv = v.shape[-1]
    kdiv, vdiv = bh // k.shape[0], bh // v.shape[0]
    tq = min(tq, t)
    assert t % tq == 0 and tq & (tq - 1) == 0
    r = g * tq
    in_specs = [pl.BlockSpec((1, g, tq, dk), lambda b, i: (b, 0, i, 0)),
                pl.BlockSpec((1, t, dk), lambda b, i: (b // kdiv, 0, 0)),
                pl.BlockSpec((1, t, dv), lambda b, i: (b // vdiv, 0, 0))]
    args = [q, k, v]
    if mode == "sb":
        in_specs.append(pl.BlockSpec((tq, tq), lambda b, i: (0, 0)))
        args.append(_suffix_ones(tq))
        kern = functools.partial(_attn_sb_kernel, tq=tq)
        scratch = [pltpu.VMEM((r, LANES), F32), pltpu.VMEM((r, dv), F32)]
    else:
        kern = functools.partial(_attn_softmax_kernel, tq=tq)
        scratch = [pltpu.VMEM((r, LANES), F32), pltpu.VMEM((r, LANES), F32), pltpu.VMEM((r, dv), F32)]
    return pl.pallas_call(kern, out_shape=jax.ShapeDtypeStruct((bh, g, t, dv), F32), grid=(bh, t // tq),
                          in_specs=in_specs, out_specs=pl.BlockSpec((1, g, tq, dv), lambda b, i: (b, 0, i, 0)),
                          scratch_shapes=scratch, compiler_params=_cparams(("arbitrary", "arbitrary")),
                          name=name)(*args)


def _pool_specs(npages, tail, layer, nchunks, reverse=False):
    specs = []
    zeros = (0,) * len(tail)
    for p in range(npages):
        if reverse:
            imap = lambda b, c, pt, p=p: (layer, pt[b, (nchunks - 1 - c) * npages + p]) + zeros
        else:
            imap = lambda b, c, pt, p=p: (layer, pt[b, c * npages + p]) + zeros
        specs.append(pl.BlockSpec((None, None) + tail, imap))
    return specs


def _softmax_init(m_scr, l_scr, acc_scr):
    m_scr[...] = jnp.full(m_scr.shape, -jnp.inf, F32)
    l_scr[...] = jnp.zeros(l_scr.shape, F32)
    acc_scr[...] = jnp.zeros(acc_scr.shape, F32)


def _softmax_update(s, v, m_scr, l_scr, acc_scr):
    m_prev = m_scr[...]
    m_new = jnp.maximum(m_prev, jnp.max(s, axis=1, keepdims=True))
    p = jnp.exp(s - m_new)
    alpha = jnp.exp(m_prev - m_new)
    l_scr[...] = alpha * l_scr[...] + jnp.sum(p, axis=1, keepdims=True)
    acc_scr[...] = alpha * acc_scr[...] + jnp.dot(p.astype(BF16), v, preferred_element_type=F32)
    m_scr[...] = m_new


def _softmax_finish(s_new, v_new, o_ref, m_scr, l_scr, acc_scr):
    m_prev = m_scr[...]
    m_fin = jnp.maximum(m_prev, s_new)
    p_new = jnp.exp(s_new - m_fin)
    alpha = jnp.exp(m_prev - m_fin)
    o_ref[0] = (alpha * acc_scr[...] + p_new * v_new) / (alpha * l_scr[...] + p_new)


def _rows(lo, n):
    return slice(lo * n, (lo + 1) * n)


def _mla_decode_kernel(pt_ref, qa_ref, qp_ref, cnew_ref, pnew_ref, *refs, npages):
    del pt_ref
    ckv_pages, kpe_pages = refs[:npages], refs[npages:2 * npages]
    o_ref, cb, pb, m_scr, l_scr, acc_scr = refs[2 * npages:]
    c = pl.program_id(1)
    pl.when(c == 0)(functools.partial(_softmax_init, m_scr, l_scr, acc_scr))
    for p in range(npages):
        cb[_rows(p, PAGE_SIZE), :] = ckv_pages[p][...].astype(BF16)
        pb[:, _rows(p, PAGE_SIZE)] = kpe_pages[p][...].astype(BF16)
    s = _qk(qa_ref[0], cb[...]) + jnp.dot(qp_ref[0], pb[...], preferred_element_type=F32)
    _softmax_update(s, cb[...], m_scr, l_scr, acc_scr)

    @pl.when(c == pl.num_programs(1) - 1)
    def _():
        s_new = (jnp.sum(qa_ref[0].astype(F32) * cnew_ref[0], axis=1, keepdims=True)
                 + jnp.sum(qp_ref[0].astype(F32) * pnew_ref[0], axis=1, keepdims=True))
        _softmax_finish(s_new, cnew_ref[0], o_ref, m_scr, l_scr, acc_scr)


def _diff_decode_kernel(pt_ref, q_ref, knew_ref, vnew_ref, *refs, npages, kv_heads):
    del pt_ref
    k_pages, v_pages = refs[:npages], refs[npages:2 * npages]
    o_ref, kb, vb, m_scr, l_scr, acc_scr = refs[2 * npages:]
    c = pl.program_id(1)
    pl.when(c == 0)(functools.partial(_softmax_init, m_scr, l_scr, acc_scr))
    ve = vb.shape[1] // kv_heads
    for p in range(npages):
        kb[:, _rows(p, PAGE_SIZE)] = k_pages[p][...].astype(BF16)
        for h in range(kv_heads):
            vb[_rows(p, PAGE_SIZE), _rows(h, ve)] = v_pages[p][pl.ds(h, PAGE_SIZE, stride=kv_heads), :].astype(BF16)
    s = jnp.dot(q_ref[0], kb[...], preferred_element_type=F32)
    _softmax_update(s, vb[...], m_scr, l_scr, acc_scr)

    @pl.when(c == pl.num_programs(1) - 1)
    def _():
        s_new = jnp.sum(q_ref[0].astype(F32) * knew_ref[0], axis=1, keepdims=True)
        _softmax_finish(s_new, vnew_ref[0], o_ref, m_scr, l_scr, acc_scr)


def _sb_decode_kernel(pt_ref, q_ref, *refs, npages):
    del pt_ref
    k_pages, v_pages = refs[:npages], refs[npages:2 * npages]
    tri_ref, o_ref, kb, vb, c_scr, acc_scr = refs[2 * npages:]
    c = pl.program_id(1)
    r = q_ref.shape[1]

    @pl.when(c == 0)
    def _():
        c_scr[...] = jnp.zeros(c_scr.shape, F32)
        acc_scr[...] = jnp.zeros(acc_scr.shape, F32)

    for p in range(npages):
        kb[:, _rows(p, PAGE_SIZE)] = k_pages[p][...].astype(BF16)
        vb[:, _rows(p, PAGE_SIZE)] = v_pages[p][...].astype(BF16)
    z = jnp.dot(q_ref[0], kb[...], preferred_element_type=F32)
    z_st = jnp.concatenate([z[:, _rows(p, PAGE_SIZE)] for p in range(npages)], axis=0)
    hi, lo = _split_bf16(_softplus(z_st))
    tri = tri_ref[...]
    sit = jnp.dot(hi, tri, preferred_element_type=F32) + jnp.dot(lo, tri, preferred_element_type=F32)
    si, tot = sit[:, :PAGE_SIZE], sit[:, PAGE_SIZE:]
    run = c_scr[...]
    later = [None] * npages
    for p in range(npages - 1, -1, -1):
        later[p] = run
        run = run + tot[_rows(p, r)]
    c_scr[...] = run
    w_st = jnp.exp(z_st - si - jnp.concatenate(later, axis=0))
    w = jnp.concatenate([w_st[_rows(p, r)] for p in range(npages)], axis=1)
    acc_scr[...] += _qk(w.astype(BF16), vb[...])

    @pl.when(c == pl.num_programs(1) - 1)
    def _():
        o_ref[0] = acc_scr[...]


def _decode_call(kern, page_table, row_args, pools, extra, r, dv, scratch, name):
    b, n_pages = page_table.shape
    npages = math.gcd(DECODE_PAGES_PER_STEP, n_pages)
    nchunks = n_pages // npages
    args = list(row_args)
    in_specs = [pl.BlockSpec((1,) + a.shape[1:], lambda i, c, pt: (i, 0, 0)) for a in row_args]
    for arr, tail, layer, reverse in pools:
        args += [arr] * npages
        in_specs += _pool_specs(npages, tail, layer, nchunks, reverse)
    for arr in extra:
        args.append(arr)
        in_specs.append(pl.BlockSpec(arr.shape, lambda i, c, pt: (0, 0)))
    grid_spec = pltpu.PrefetchScalarGridSpec(
        num_scalar_prefetch=1, grid=(b, nchunks), in_specs=in_specs,
        out_specs=pl.BlockSpec((1, r, dv), lambda i, c, pt: (i, 0, 0)), scratch_shapes=scratch(npages * PAGE_SIZE))
    return pl.pallas_call(functools.partial(kern, npages=npages), out_shape=jax.ShapeDtypeStruct((b, r, dv), F32),
                          grid_spec=grid_spec, compiler_params=_cparams(("arbitrary", "arbitrary")),
                          name=name)(page_table, *args)


def _softmax_scratch(r, dv):
    return [pltpu.VMEM((r, 1), F32), pltpu.VMEM((r, 1), F32), pltpu.VMEM((r, dv), F32)]


def _headwise_kernel(x_ref, w_ref, o_ref):
    h, din, dout = w_ref.shape
    for i in range(h):
        o_ref[:, i * dout:(i + 1) * dout] = jnp.dot(x_ref[:, i * din:(i + 1) * din].astype(BF16), w_ref[i],
                                                    preferred_element_type=F32)


def _headwise(x, w, name):
    n = x.shape[0]
    h, din, dout = w.shape
    return pl.pallas_call(_headwise_kernel, out_shape=jax.ShapeDtypeStruct((n, h * dout), F32),
                          name=name)(x, w.astype(BF16))


def _moe_kernel(x_ref, g_ref, wrh_ref, wrl_ref, br_ref, wg_ref, wu_ref, wd_ref, fg_ref, o_ref,
                xn_scr, comb_scr, acc_scr, *, final_norm):
    e = pl.program_id(1)
    tm = x_ref.shape[0]
    lane = lax.broadcasted_iota(jnp.int32, (tm, LANES), 1)

    @pl.when(e == 0)
    def _():
        x = x_ref[...]
        xn = x * lax.rsqrt(jnp.mean(x * x, axis=-1, keepdims=True) + NORM_EPS) * g_ref[...]
        xh, xl = _split_bf16(xn)
        xn_scr[...] = xh
        wrh = wrh_ref[...]
        lg = (jnp.dot(xh, wrh, preferred_element_type=F32) + jnp.dot(xl, wrh, preferred_element_type=F32)
              + jnp.dot(xh, wrl_ref[...], preferred_element_type=F32) + br_ref[...])
        big = jnp.int32(LANES)
        is_g = jnp.logical_and(lane >= N_EXPERTS, lane < N_EXPERTS + N_GROUPS)
        gl = jnp.where(is_g, lg, -jnp.inf)
        gmax = jnp.max(gl, axis=1, keepdims=True)
        gw = 1.0 / jnp.sum(jnp.exp(gl - gmax), axis=1, keepdims=True)
        gsel = jnp.min(jnp.where(gl == gmax, lane, big), axis=1, keepdims=True) - N_EXPERTS
        in_grp = jnp.logical_and(lane < N_EXPERTS, (lane >> 3) == gsel)
        el = jnp.where(in_grp, lg, -jnp.inf)
        v1 = jnp.max(el, axis=1, keepdims=True)
        i1 = jnp.min(jnp.where(el == v1, lane, big), axis=1, keepdims=True)
        el2 = jnp.where(lane == i1, -jnp.inf, el)
        v2 = jnp.max(el2, axis=1, keepdims=True)
        i2 = jnp.min(jnp.where(el2 == v2, lane, big), axis=1, keepdims=True)
        e2 = jnp.exp(v2 - v1)
        w1 = gw / (1.0 + e2)
        comb_scr[...] = jnp.where(lane == i1, w1, 0.0) + jnp.where(lane == i2, w1 * e2, 0.0)
        acc_scr[...] = jnp.zeros(acc_scr.shape, F32)

    xb = xn_scr[...]
    h = jnp.dot(xb, wg_ref[0], preferred_element_type=F32)
    u = jnp.dot(xb, wu_ref[0], preferred_element_type=F32)
    ce = jnp.sum(jnp.where(lane == e, comb_scr[...], 0.0), axis=1, keepdims=True)
    act = h * (1.0 / (1.0 + jnp.exp(-h))) * u * ce
    acc_scr[...] += jnp.dot(act.astype(BF16), wd_ref[0], preferred_element_type=F32)

    @pl.when(e == pl.num_programs(1) - 1)
    def _():
        y = x_ref[...] + acc_scr[...]
        if final_norm:
            y = y * lax.rsqrt(jnp.mean(y * y, axis=-1, keepdims=True) + NORM_EPS) * fg_ref[...]
        o_ref[...] = y


def _moe(x, norm_g, w_group, b_group, w_router, b_router, w_gate, w_up, w_down, final_g, name):
    n, d = x.shape
    ne, _, f = w_gate.shape
    tm = _row_tile(n, 1024)
    wr = jnp.zeros((d, LANES), F32)
    wr = wr.at[:, :N_EXPERTS].set(w_router.reshape(d, N_EXPERTS)).at[:, N_EXPERTS:N_EXPERTS + N_GROUPS].set(w_group)
    br = jnp.zeros((1, LANES), F32)
    br = br.at[0, :N_EXPERTS].set(b_router.reshape(N_EXPERTS)).at[0, N_EXPERTS:N_EXPERTS + N_GROUPS].set(b_group)
    wrh = wr.astype(BF16)
    wrl = (wr - wrh.astype(F32)).astype(BF16)
    fg = jnp.ones((1, d), F32) if final_g is None else final_g.reshape(1, d).astype(F32)
    row = lambda i, e: (i, 0)
    const = lambda i, e: (0, 0)
    in_specs = [pl.BlockSpec((tm, d), row), pl.BlockSpec((1, d), const), pl.BlockSpec((d, LANES), const),
                pl.BlockSpec((d, LANES), const), pl.BlockSpec((1, LANES), const),
                pl.BlockSpec((1, d, f), lambda i, e: (e, 0, 0)), pl.BlockSpec((1, d, f), lambda i, e: (e, 0, 0)),
                pl.BlockSpec((1, f, d), lambda i, e: (e, 0, 0)), pl.BlockSpec((1, d), const)]
    return pl.pallas_call(
        functools.partial(_moe_kernel, final_norm=final_g is not None),
        out_shape=jax.ShapeDtypeStruct((n, d), F32), grid=(n // tm, ne), in_specs=in_specs,
        out_specs=pl.BlockSpec((tm, d), row),
        scratch_shapes=[pltpu.VMEM((tm, d), BF16), pltpu.VMEM((tm, LANES), F32), pltpu.VMEM((tm, d), F32)],
        compiler_params=_cparams(("arbitrary", "arbitrary")), name=name,
    )(x, norm_g.reshape(1, d).astype(F32), wrh, wrl, br, w_gate.astype(BF16), w_up.astype(BF16),
      w_down.astype(BF16), fg)


def _rwkv_slot(s, idx, lane, r_ref, w_ref, k_ref, kk_ref, kka_ref, vt_ref, y_scr, c):
    row = lambda ref: ref[c, pl.ds(idx, 1), :]
    sel = lane == idx
    vcol = jnp.sum(jnp.where(sel, vt_ref[c], 0.0), axis=1, keepdims=True)
    sa = -jnp.sum(s * row(kk_ref), axis=1, keepdims=True)
    s = s * row(w_ref) + sa * row(kka_ref) + vcol * row(k_ref)
    y = jnp.sum(s * row(r_ref), axis=1, keepdims=True)
    y_scr[c] = jnp.where(sel, y, y_scr[c])
    return s


def _rwkv_scan_kernel(r_ref, w_ref, k_ref, kk_ref, kka_ref, vt_ref, yt_ref, sfin_ref, s_scr, y_scr, beta_scr,
                      gamma_scr):
    tblk = pl.program_id(0)
    nch, tb, hs = r_ref.shape

    @pl.when(tblk == 0)
    def _():
        s_scr[...] = jnp.zeros(s_scr.shape, F32)

    lane = lax.broadcasted_iota(jnp.int32, (1, hs, tb), 2)
    beta_scr[...] = jnp.broadcast_to(jnp.sum(kka_ref[...] * r_ref[...], axis=2, keepdims=True), beta_scr.shape)
    gamma_scr[...] = jnp.broadcast_to(jnp.sum(k_ref[...] * r_ref[...], axis=2, keepdims=True), gamma_scr.shape)

    def step(t, carry):
        row = lambda ref: ref[:, pl.ds(t, 1), :]
        r, w, k, kk, kka = row(r_ref), row(w_ref), row(k_ref), row(kk_ref), row(kka_ref)
        sel = lane == t
        s = s_scr[...]
        vcol = jnp.sum(jnp.where(sel, vt_ref[...], 0.0), axis=2, keepdims=True)
        sa = -jnp.sum(s * kk, axis=2, keepdims=True)
        y0 = jnp.sum(s * (w * r), axis=2, keepdims=True)
        s_scr[...] = s * w + sa * kka + vcol * k
        y = y0 + sa * row(beta_scr) + vcol * row(gamma_scr)
        pltpu.store(y_scr, y, mask=jnp.broadcast_to(sel, y_scr.shape))
        return carry

    lax.fori_loop(0, tb, step, 0)
    yt_ref[...] = y_scr[...]

    @pl.when(tblk == pl.num_programs(0) - 1)
    def _():
        sfin_ref[...] = s_scr[...]


def _rwkv_scan(r, w, k, kk, kka, vt):
    nch, t, hs = r.shape
    tb = min(t, LANES)
    assert t % tb == 0
    rows = pl.BlockSpec((nch, tb, hs), lambda i: (0, i, 0))
    cols = pl.BlockSpec((nch, hs, tb), lambda i: (0, 0, i))
    return pl.pallas_call(
        _rwkv_scan_kernel,
        out_shape=[jax.ShapeDtypeStruct((nch, hs, t), F32), jax.ShapeDtypeStruct((nch, hs, hs), F32)],
        grid=(t // tb,), in_specs=[rows] * 5 + [cols],
        out_specs=[cols, pl.BlockSpec((nch, hs, hs), lambda i: (0, 0, 0))],
        scratch_shapes=[pltpu.VMEM((nch, hs, hs), F32), pltpu.VMEM((nch, hs, tb), F32),
                        pltpu.VMEM((nch, tb, tb), F32), pltpu.VMEM((nch, tb, tb), F32)],
        compiler_params=_cparams(("arbitrary",)), name="rwkv_scan")(r, w, k, kk, kka, vt)


def _rwkv_step_kernel(s_ref, r_ref, w_ref, k_ref, kk_ref, kka_ref, vt_ref, so_ref, yt_ref, y_scr):
    nb, hs = r_ref.shape[1], r_ref.shape[2]
    lane = lax.broadcasted_iota(jnp.int32, (hs, nb), 1)
    y_scr[...] = jnp.zeros(y_scr.shape, F32)

    def step(b, carry):
        so_ref[b, 0] = _rwkv_slot(s_ref[b, 0], b, lane, r_ref, w_ref, k_ref, kk_ref, kka_ref, vt_ref, y_scr, 0)
        return carry

    lax.fori_loop(0, nb, step, 0)
    yt_ref[...] = y_scr[...]


def _rwkv_step(s, r, w, k, kk, kka, vt):
    b, h, hs, _ = s.shape
    nb = min(b, LANES)
    assert b % nb == 0
    sspec = pl.BlockSpec((nb, 1, hs, hs), lambda j, i: (i, j, 0, 0))
    rows = pl.BlockSpec((1, nb, hs), lambda j, i: (j, i, 0))
    cols = pl.BlockSpec((1, hs, nb), lambda j, i: (j, 0, i))
    return pl.pallas_call(
        _rwkv_step_kernel,
        out_shape=[jax.ShapeDtypeStruct(s.shape, F32), jax.ShapeDtypeStruct((h, hs, b), F32)],
        grid=(h, b // nb), in_specs=[sspec] + [rows] * 5 + [cols], out_specs=[sspec, cols],
        scratch_shapes=[pltpu.VMEM((1, hs, nb), F32)],
        compiler_params=_cparams(("arbitrary", "arbitrary")), name="rwkv_step")(s, r, w, k, kk, kka, vt)


def _rope_tables(pos, reps):
    half = MLA_ROPE // 2
    inv = ROPE_BASE ** (-jnp.arange(half, dtype=F32) / half)
    ang = pos.astype(F32)[:, None] * inv[None, :]
    cos, sin = jnp.cos(ang), jnp.sin(ang)
    return jnp.tile(jnp.concatenate([cos, cos], -1), (1, reps)), jnp.tile(jnp.concatenate([-sin, sin], -1), (1, reps))


def _swap_halves(w, width):
    k, m = w.shape
    w = w.reshape(k, m // width, 2, width // 2)
    return jnp.concatenate([w[:, :, 1], w[:, :, 0]], axis=2).reshape(k, m)


def _even_weights(w_in, q_norm, w_uq, kv_norm, w_uk, w_uv, w_out):
    kpe_lo = MLA_Q_LORA + MLA_KV_LORA
    w_in_ext = jnp.concatenate([w_in, _swap_halves(w_in[:, kpe_lo:kpe_lo + MLA_ROPE], MLA_ROPE)], axis=1)
    wq = w_uq.reshape(MLA_Q_LORA, MLA_HEADS, MLA_NOPE + MLA_ROPE)
    wq_n = wq[:, :, :MLA_NOPE].reshape(MLA_Q_LORA, -1)
    wq_p = wq[:, :, MLA_NOPE:].reshape(MLA_Q_LORA, -1)
    wq_ext = jnp.concatenate([wq_n, wq_p, _swap_halves(wq_p, MLA_ROPE)], axis=1)
    w_kv = jnp.concatenate([w_uk.reshape(MLA_KV_LORA, -1), w_uv.reshape(MLA_KV_LORA, -1)], axis=1)
    return dict(w_in=w_in_ext, q_norm=q_norm, wq=wq_ext, kv_norm=kv_norm, w_kv=w_kv, w_uk=w_uk, w_uv=w_uv,
                w_out=w_out)


def _even_project(h_in, attn_g, wts, pos, b, t):
    n = h_in.shape[0]
    proj = _linear(h_in, wts["w_in"], g=attn_g, name="even_in")
    o = 0
    parts = []
    for wdt in (MLA_Q_LORA, MLA_KV_LORA, MLA_ROPE, SB_HEADS * SB_HD, SB_KV_HEADS * SB_HD, SB_KV_HEADS * SB_HD,
                MLA_ROPE):
        parts.append(proj[:, o:o + wdt])
        o += wdt
    q_lat, kv_lat, kpe_raw, sq, sk, sv, kpe_sw = parts
    cos1, sin1 = _rope_tables(pos, 1)
    cos1, sin1 = jnp.tile(cos1, (b, 1)), jnp.tile(sin1, (b, 1))
    kpe = kpe_raw * cos1 + kpe_sw * sin1
    qall = _linear(q_lat, wts["wq"], g=wts["q_norm"], name="mla_q_up")
    hp = MLA_HEADS * MLA_ROPE
    q_nope, q_pe, q_pe_sw = qall[:, :MLA_HEADS * MLA_NOPE], qall[:, -2 * hp:-hp], qall[:, -hp:]
    q_pe = q_pe * jnp.tile(cos1, (1, MLA_HEADS)) + q_pe_sw * jnp.tile(sin1, (1, MLA_HEADS))
    return q_nope, q_pe, kv_lat, kpe, sq, sk, sv


def _heads_first(x, b, t, h):
    d = x.shape[1] // h
    return x.reshape(b, t, h, d).transpose(0, 2, 1, 3).reshape(b * h, t, d)


def _even_prompt(x, attn_g, wts, b, t):
    pos = jnp.arange(t, dtype=jnp.int32)
    q_nope, q_pe, kv_lat, kpe, sq, sk, sv = _even_project(x, attn_g, wts, pos, b, t)
    kexp, ckv = _linear(kv_lat, wts["w_kv"], g=wts["kv_norm"], emit_normed=True, name="mla_kv_up")
    hn = MLA_HEADS * MLA_NOPE
    scale = (MLA_NOPE + MLA_ROPE) ** -0.5
    q = jnp.concatenate([_heads_first(q_nope, b, t, MLA_HEADS), _heads_first(q_pe, b, t, MLA_HEADS)], axis=-1)
    q = (q * scale).astype(BF16)[:, None]
    k = jnp.concatenate([_heads_first(kexp[:, :hn], b, t, MLA_HEADS),
                         jnp.broadcast_to(kpe.reshape(b, 1, t, MLA_ROPE), (b, MLA_HEADS, t, MLA_ROPE)
                                          ).reshape(b * MLA_HEADS, t, MLA_ROPE)], axis=-1).astype(BF16)
    v = _heads_first(kexp[:, hn:], b, t, MLA_HEADS).astype(BF16)
    mla = _causal_attention(q, k, v, "softmax", 512, "mla_prompt")
    mla = mla.reshape(b, MLA_HEADS, t, MLA_V).transpose(0, 2, 1, 3).reshape(b * t, MLA_HEADS * MLA_V)
    grp = SB_HEADS // SB_KV_HEADS
    sq_h = (sq * SB_HD ** -0.5).astype(BF16).reshape(b, t, SB_KV_HEADS, grp, SB_HD).transpose(0, 2, 3, 1, 4)
    sq_h = sq_h.reshape(b * SB_KV_HEADS, grp, t, SB_HD)
    sbo = _causal_attention(sq_h, _heads_first(sk, b, t, SB_KV_HEADS).astype(BF16),
                            _heads_first(sv, b, t, SB_KV_HEADS).astype(BF16), "sb", 256, "sb_prompt")
    sbo = sbo.reshape(b, SB_KV_HEADS, grp, t, SB_HD).transpose(0, 3, 1, 2, 4).reshape(b * t, SB_HEADS * SB_HD)
    y = _outproj(mla, sbo, wts["w_out"], x, name="even_out")
    return y, (ckv, kpe, sk, sv)


def _feature_major(cache):
    nl, npool, page = cache.shape[:3]
    nd = cache.ndim
    return cache.transpose((0, 1) + tuple(range(3, nd)) + (2,)).reshape(nl, npool, -1, page)


def _even_sample(x, attn_g, wts, page_table, layer, c_ckv, c_kpe, c_sk, c_sv):
    b = x.shape[0]
    past_len = page_table.shape[1] * PAGE_SIZE
    pos = jnp.full((1,), past_len, jnp.int32)
    q_nope, q_pe, kv_lat, kpe, sq, sk, sv = _even_project(x, attn_g, wts, pos, b, 1)
    _, ckv = _linear(kv_lat, wts["w_kv"], g=wts["kv_norm"], emit_normed=True, name="mla_kv_norm_s")
    scale = (MLA_NOPE + MLA_ROPE) ** -0.5
    q_abs = _headwise(q_nope, wts["w_uk"].transpose(1, 2, 0), "mla_q_abs")
    q_abs = (q_abs * scale).astype(BF16).reshape(b, MLA_HEADS, MLA_KV_LORA)
    q_pe = (q_pe * scale).astype(BF16).reshape(b, MLA_HEADS, MLA_ROPE)
    lat = _decode_call(
        _mla_decode_kernel, page_table, [q_abs, q_pe, ckv[:, None], kpe[:, None]],
        [(c_ckv, (PAGE_SIZE, MLA_KV_LORA), layer, False), (_feature_major(c_kpe), (MLA_ROPE, PAGE_SIZE), layer, False)],
        [], MLA_HEADS, MLA_KV_LORA,
        lambda tk: [pltpu.VMEM((tk, MLA_KV_LORA), BF16), pltpu.VMEM((MLA_ROPE, tk), BF16)]
        + _softmax_scratch(MLA_HEADS, MLA_KV_LORA), "mla_decode")
    mla = _headwise(lat.reshape(b, MLA_HEADS * MLA_KV_LORA), wts["w_uv"].transpose(1, 0, 2), "mla_v_up")
    kvw = SB_KV_HEADS * SB_HD
    grp = SB_HEADS // SB_KV_HEADS
    sq_rows = (sq * SB_HD ** -0.5).reshape(b, SB_KV_HEADS, grp, 1, SB_HD) * jnp.eye(SB_KV_HEADS, dtype=F32)[
        None, :, None, :, None]
    sq_rows = sq_rows.reshape(b, SB_HEADS, kvw).astype(BF16)
    tri = jnp.concatenate([_suffix_ones(PAGE_SIZE), jnp.ones((PAGE_SIZE, PAGE_SIZE), BF16)], axis=1)
    sbo = _decode_call(
        _sb_decode_kernel, page_table, [sq_rows],
        [(_feature_major(c_sk), (kvw, PAGE_SIZE), layer, True), (_feature_major(c_sv), (kvw, PAGE_SIZE), layer, True)],
        [tri], SB_HEADS, kvw,
        lambda tk: [pltpu.VMEM((kvw, tk), BF16), pltpu.VMEM((kvw, tk), BF16), pltpu.VMEM((SB_HEADS, LANES), F32),
                    pltpu.VMEM((SB_HEADS, kvw), F32)], "sb_decode")
    sbo = sbo.reshape(b, SB_KV_HEADS, grp, SB_KV_HEADS, SB_HD)
    sbo = jnp.einsum("bkgjd,kj->bkgd", sbo, jnp.eye(SB_KV_HEADS, dtype=F32)).reshape(b, SB_HEADS * SB_HD)
    y = _outproj(mla, sbo, wts["w_out"], x, name="even_out_s")
    return y, (ckv, kpe, sk, sv)


def _rwkv_prepare(rw, prev, p):
    xs = rw + (prev - rw) * p["mu"]
    o = 0
    parts = []
    for wdt in (RW_DIM, RW_DIM, RW_DIM, RW_W_LORA, RW_A_LORA, RW_G_LORA):
        parts.append(xs[:, o:o + wdt])
        o += wdt
    r, k, v, xw, xa, xg = parts
    w_log = -jax.nn.softplus(-(p["w0"] + _linear(jnp.tanh(xw), p["w2"], name="rw_w_lora"))) - 0.5
    decay = jnp.exp(-jnp.exp(w_log))
    a = jax.nn.sigmoid(p["a0"] + _linear(xa, p["a2"], name="rw_a_lora"))
    g = _linear(jax.nn.sigmoid(xg), p["g2"], name="rw_g_lora")
    n = r.shape[0]
    kk = (k * p["kk"]).reshape(n, RW_HEADS, RW_HS)
    kk = (kk * lax.rsqrt(jnp.maximum(jnp.sum(kk * kk, axis=-1, keepdims=True), 1e-24))).reshape(n, RW_DIM)
    k = k * (1.0 + (a - 1.0) * p["ka"])
    return r, decay, k, v, kk, kk * a, g


def _rwkv_finish(yr, r, k, v, g, p):
    n = yr.shape[0]
    yh = yr.reshape(n, RW_HEADS, RW_HS)
    mean = jnp.mean(yh, axis=-1, keepdims=True)
    var = jnp.mean(jnp.square(yh - mean), axis=-1, keepdims=True)
    yn = ((yh - mean) * lax.rsqrt(var + RW_LN_EPS)).reshape(n, RW_DIM) * p["ln_w"] + p["ln_b"]
    rk = (r * k * p["rk"].reshape(1, RW_DIM)).reshape(n, RW_HEADS, RW_HS)
    bonus = (jnp.sum(rk, axis=-1, keepdims=True) * v.reshape(n, RW_HEADS, RW_HS)).reshape(n, RW_DIM)
    return (yn + bonus) * g


def _odd_split(proj):
    dq_w = DF_HEADS * 2 * DF_HD
    dkv_w = DF_KV_HEADS * 2 * DF_HD
    return proj[:, :dq_w], proj[:, dq_w:dq_w + dkv_w], proj[:, dq_w + dkv_w:dq_w + 2 * dkv_w], proj[:, dq_w + 2 * dkv_w:]


def _odd_prompt(x, attn_g, p, lam, lam_init, b, t):
    n = x.shape[0]
    proj = _linear(x, p["w_in"], g=attn_g, name="odd_in")
    dq, dk, dv, rw = _odd_split(proj)
    grp = DF_HEADS // DF_KV_HEADS
    q = (dq * DF_HD ** -0.5).astype(BF16).reshape(b, t, DF_KV_HEADS, grp, 2, DF_HD).transpose(0, 2, 4, 3, 1, 5)
    q = q.reshape(b * DF_KV_HEADS * 2, grp, t, DF_HD)
    k = dk.astype(BF16).reshape(b, t, DF_KV_HEADS * 2, DF_HD).transpose(0, 2, 1, 3).reshape(-1, t, DF_HD)
    v = _heads_first(dv, b, t, DF_KV_HEADS).astype(BF16)
    o = _causal_attention(q, k, v, "softmax", 512, "diff_prompt")
    o = o.reshape(b, DF_KV_HEADS, 2, grp, t, 2 * DF_HD).transpose(2, 0, 4, 1, 3, 5).reshape(2, n, DF_HEADS * 2 * DF_HD)
    rwb = rw.reshape(b, t, -1)
    prev = jnp.concatenate([jnp.zeros((b, 1, rwb.shape[-1]), F32), rwb[:, :-1]], axis=1).reshape(n, -1)
    r, decay, k_, v_, kk, kka, g = _rwkv_prepare(rw, prev, p)
    hf = lambda z: _heads_first(z, b, t, RW_HEADS)
    vt = hf(v_).transpose(0, 2, 1)
    yt, s_fin = _rwkv_scan(hf(r), hf(decay), hf(k_), hf(kk), hf(kka), vt)
    yr = yt.reshape(b, RW_HEADS, RW_HS, t).transpose(0, 3, 1, 2).reshape(n, RW_DIM)
    rw_out = _rwkv_finish(yr, r, k_, v_, g, p)
    y = _outproj((o[0], o[1]), rw_out, p["w_out"], x, diff=(lam, p["subln"], 1.0 - lam_init, DF_HEADS),
                 name="odd_out")
    s_fin = s_fin.reshape(b, RW_HEADS, RW_HS, RW_HS)
    return y, (dk, dv), s_fin, rwb[:, -1]


def _odd_sample(x, attn_g, p, lam, lam_init, page_table, layer, c_dk, c_dv, shift_prev, s_prev):
    b = x.shape[0]
    proj = _linear(x, p["w_in"], g=attn_g, name="odd_in_s")
    dq, dk, dv, rw = _odd_split(proj)
    nl, npool = c_dk.shape[:2]
    kw = DF_KV_HEADS * 2 * DF_HD
    vw = DF_KV_HEADS * 2 * DF_HD
    q_rows = _block_rows_kgc(dq * DF_HD ** -0.5).astype(BF16)
    nrow = DF_HEADS * 2
    o = _decode_call(
        functools.partial(_diff_decode_kernel, kv_heads=DF_KV_HEADS), page_table, [q_rows, dk[:, None], dv[:, None]],
        [(_feature_major(c_dk), (kw, PAGE_SIZE), layer, False),
         (c_dv.reshape(nl, npool, PAGE_SIZE * DF_KV_HEADS, 2 * DF_HD), (PAGE_SIZE * DF_KV_HEADS, 2 * DF_HD), layer,
          False)],
        [], nrow, vw,
        lambda tk: [pltpu.VMEM((kw, tk), BF16), pltpu.VMEM((tk, vw), BF16)] + _softmax_scratch(nrow, vw),
        "diff_decode")
    grp = DF_HEADS // DF_KV_HEADS
    o = o.reshape(b, DF_KV_HEADS, grp, 2, DF_KV_HEADS, 2 * DF_HD)
    o = jnp.einsum("bkgcje,kj->cbkge", o, jnp.eye(DF_KV_HEADS, dtype=F32)).reshape(2, b, DF_HEADS * 2 * DF_HD)
    r, decay, k_, v_, kk, kka, g = _rwkv_prepare(rw, shift_prev, p)
    hb = lambda z: z.reshape(b, RW_HEADS, RW_HS).transpose(1, 0, 2)
    s_new, yt = _rwkv_step(s_prev, hb(r), hb(decay), hb(k_), hb(kk), hb(kka), hb(v_).transpose(0, 2, 1))
    yr = yt.transpose(2, 0, 1).reshape(b, RW_DIM)
    rw_out = _rwkv_finish(yr, r, k_, v_, g, p)
    y = _outproj((o[0], o[1]), rw_out, p["w_out"], x, diff=(lam, p["subln"], 1.0 - lam_init, DF_HEADS),
                 name="odd_out_s")
    return y, (dk, dv), s_new, rw


def _block_rows_kgc(dq):
    b = dq.shape[0]
    grp = DF_HEADS // DF_KV_HEADS
    q = dq.reshape(b, DF_KV_HEADS, grp, 2, 1, 1, DF_HD)
    eye_k = jnp.eye(DF_KV_HEADS, dtype=dq.dtype)[None, :, None, None, :, None, None]
    eye_c = jnp.eye(2, dtype=dq.dtype)[None, None, None, :, None, :, None]
    return (q * eye_k * eye_c).reshape(b, DF_KV_HEADS * grp * 2, DF_KV_HEADS * 2 * DF_HD)


def kernel(x_prompt, x_sample, cache_mla_ckv, cache_mla_kpe, cache_sb_k, cache_sb_v, cache_df_k, cache_df_v, state_rw_s, state_rw_shift, page_table, attn_norm, ffn_norm, final_norm, ev_w_in, mla_q_norm, mla_w_uq, mla_kv_norm, mla_w_uk, mla_w_uv, ev_w_out, od_w_in, df_lq1, df_lk1, df_lq2, df_lk2, df_subln, rw_mu, rw_w0, rw_w2, rw_a0, rw_a2, rw_g2, rw_kk, rw_ka, rw_rk, rw_ln_w, rw_ln_b, od_w_out, moe_w_group, moe_b_group, moe_w_router, moe_b_router, moe_w_gate, moe_w_up, moe_w_down):
    bp, tp, d = x_prompt.shape
    bs, ts, _ = x_sample.shape
    assert ts == 1
    depth = attn_norm.shape[0]
    xp = x_prompt.reshape(bp * tp, d)
    xs = x_sample.reshape(bs, d)
    outs = {k: [] for k in ("ckv_p", "ckv_s", "kpe_p", "kpe_s", "sbk_p", "sbk_s", "sbv_p", "sbv_s", "dfk_p", "dfk_s",
                            "dfv_p", "dfv_s", "rws_p", "rws_s", "rwsh_p", "rwsh_s")}
    for layer in range(depth):
        i = layer // 2
        if layer % 2 == 0:
            wts = _even_weights(ev_w_in[i], mla_q_norm[i], mla_w_uq[i], mla_kv_norm[i], mla_w_uk[i], mla_w_uv[i],
                                ev_w_out[i])
            xp, rp = _even_prompt(xp, attn_norm[layer], wts, bp, tp)
            xs, rs = _even_sample(xs, attn_norm[layer], wts, page_table, i, cache_mla_ckv, cache_mla_kpe,
                                  cache_sb_k, cache_sb_v)
            outs["ckv_p"].append(rp[0].reshape(bp, tp, MLA_KV_LORA))
            outs["kpe_p"].append(rp[1].reshape(bp, tp, MLA_ROPE))
            outs["sbk_p"].append(rp[2].reshape(bp, tp, SB_KV_HEADS, SB_HD))
            outs["sbv_p"].append(rp[3].reshape(bp, tp, SB_KV_HEADS, SB_HD))
            outs["ckv_s"].append(rs[0].reshape(bs, ts, MLA_KV_LORA))
            outs["kpe_s"].append(rs[1].reshape(bs, ts, MLA_ROPE))
            outs["sbk_s"].append(rs[2].reshape(bs, ts, SB_KV_HEADS, SB_HD))
            outs["sbv_s"].append(rs[3].reshape(bs, ts, SB_KV_HEADS, SB_HD))
        else:
            lam_init = 0.8 - 0.6 * math.exp(-0.3 * layer)
            lam = (jnp.exp(jnp.sum(df_lq1[i] * df_lk1[i])) - jnp.exp(jnp.sum(df_lq2[i] * df_lk2[i])) + lam_init)
            p = dict(w_in=od_w_in[i], subln=df_subln[i], mu=rw_mu[i], w0=rw_w0[i], w2=rw_w2[i], a0=rw_a0[i],
                     a2=rw_a2[i], g2=rw_g2[i], kk=rw_kk[i], ka=rw_ka[i], rk=rw_rk[i], ln_w=rw_ln_w[i],
                     ln_b=rw_ln_b[i], w_out=od_w_out[i])
            xp, kvp, sp, shp = _odd_prompt(xp, attn_norm[layer], p, lam, lam_init, bp, tp)
            xs, kvs, ss, shs = _odd_sample(xs, attn_norm[layer], p, lam, lam_init, page_table, i, cache_df_k,
                                           cache_df_v, state_rw_shift[i], state_rw_s[i])
            outs["dfk_p"].append(kvp[0].reshape(bp, tp, DF_KV_HEADS, 2, DF_HD))
            outs["dfv_p"].append(kvp[1].reshape(bp, tp, DF_KV_HEADS, 2 * DF_HD))
            outs["dfk_s"].append(kvs[0].reshape(bs, ts, DF_KV_HEADS, 2, DF_HD))
            outs["dfv_s"].append(kvs[1].reshape(bs, ts, DF_KV_HEADS, 2 * DF_HD))
            outs["rws_p"].append(sp)
            outs["rws_s"].append(ss)
            outs["rwsh_p"].append(shp)
            outs["rwsh_s"].append(shs)
        fin = final_norm if layer == depth - 1 else None
        mw = (moe_w_group[layer], moe_b_group[layer], moe_w_router[layer], moe_b_router[layer], moe_w_gate[layer],
              moe_w_up[layer], moe_w_down[layer])
        xp = _moe(xp, ffn_norm[layer], *mw, fin, "moe_p")
        xs = _moe(xs, ffn_norm[layer], *mw, fin, "moe_s")
    st = lambda key: jnp.stack(outs[key], axis=0)
    return (xp.reshape(bp, tp, d), xs.reshape(bs, ts, d), st("ckv_p"), st("ckv_s"), st("kpe_p"), st("kpe_s"),
            st("sbk_p"), st("sbk_s"), st("sbv_p"), st("sbv_s"), st("dfk_p"), st("dfk_s"), st("dfv_p"), st("dfv_s"),
            st("rws_p"), st("rws_s"), st("rwsh_p"), st("rwsh_s"))
```

```python
import functools
import math

import jax
import jax.numpy as jnp
from jax import lax
from jax.experimental import pallas as pl
from jax.experimental.pallas import tpu as pltpu

F32 = jnp.float32
BF16 = jnp.bfloat16

PAGE_SIZE = 128
NORM_EPS = 1e-6
MLA_HEADS, MLA_Q_LORA, MLA_KV_LORA, MLA_NOPE, MLA_ROPE, MLA_V = 8, 384, 256, 64, 32, 64
ROPE_BASE = 10000.0
SB_HEADS, SB_KV_HEADS, SB_HD = 8, 2, 64
DF_HEADS, DF_KV_HEADS, DF_HD = 4, 2, 64
RW_HEADS, RW_HS = 8, 64
RW_DIM = RW_HEADS * RW_HS
RW_W_LORA, RW_A_LORA, RW_G_LORA = 64, 64, 128
RW_LN_EPS = 64e-5
N_GROUPS, EXPERTS_PER_GROUP = 4, 8
N_EXPERTS = N_GROUPS * EXPERTS_PER_GROUP
LANES = 128
VMEM_LIMIT = 48 * 1024 * 1024
DECODE_PAGES_PER_STEP = 64
DECODE_SEQS_PER_STEP = 1


def _cparams(sem):
    return pltpu.CompilerParams(dimension_semantics=sem, vmem_limit_bytes=VMEM_LIMIT)


def _row_tile(n, pref):
    t = min(n, pref)
    while n % t:
        t //= 2
    return t


def _linear_kernel(*refs, norm, has_res, emit_normed):
    it = iter(refs)
    x_ref = next(it)
    g_ref = next(it) if norm else None
    w_ref = next(it)
    res_ref = next(it) if has_res else None
    o_ref = next(it)
    xn_ref = next(it) if emit_normed else None
    x = x_ref[...].astype(F32)
    if norm:
        x = x * lax.rsqrt(jnp.mean(x * x, axis=-1, keepdims=True) + NORM_EPS) * g_ref[...]
        if emit_normed:
            xn_ref[...] = x
    y = jnp.dot(x.astype(BF16), w_ref[...], preferred_element_type=F32)
    if has_res:
        y = y + res_ref[...]
    o_ref[...] = y.astype(o_ref.dtype)


def _linear(x, w, g=None, res=None, emit_normed=False, name="linear"):
    n, k = x.shape
    m = w.shape[1]
    tm = _row_tile(n, 512)
    tn = m if (m <= 2048 or emit_normed) else m // 2
    assert m % tn == 0 and (tn == m or tn % LANES == 0)
    grid = (m // tn, n // tm)
    in_specs = [pl.BlockSpec((tm, k), lambda j, i: (i, 0))]
    args = [x]
    if g is not None:
        in_specs.append(pl.BlockSpec((1, k), lambda j, i: (0, 0)))
        args.append(g.reshape(1, k).astype(F32))
    in_specs.append(pl.BlockSpec((k, tn), lambda j, i: (0, j)))
    args.append(w.astype(BF16))
    if res is not None:
        in_specs.append(pl.BlockSpec((tm, tn), lambda j, i: (i, j)))
        args.append(res)
    out_shape = [jax.ShapeDtypeStruct((n, m), F32)]
    out_specs = [pl.BlockSpec((tm, tn), lambda j, i: (i, j))]
    if emit_normed:
        out_shape.append(jax.ShapeDtypeStruct((n, k), F32))
        out_specs.append(pl.BlockSpec((tm, k), lambda j, i: (i, 0)))
    outs = pl.pallas_call(
        functools.partial(_linear_kernel, norm=g is not None, has_res=res is not None, emit_normed=emit_normed),
        out_shape=out_shape, grid=grid, in_specs=in_specs, out_specs=out_specs,
        compiler_params=_cparams(("arbitrary", "arbitrary")), name=name)(*args)
    return outs if emit_normed else outs[0]


def _outproj_kernel(*refs, diff_heads, diff_scale):
    if diff_heads:
        a0_ref, a1_ref, lam_ref, sub_ref, b_ref, wa_ref, wb_ref, x_ref, o_ref = refs
        lam = lam_ref[0, 0]
        hd = a0_ref.shape[1] // diff_heads
        y = x_ref[...] + jnp.dot(b_ref[...].astype(BF16), wb_ref[...], preferred_element_type=F32)
        for h in range(diff_heads):
            sl = slice(h * hd, (h + 1) * hd)
            o = a0_ref[:, sl] - lam * a1_ref[:, sl]
            o = o * lax.rsqrt(jnp.mean(o * o, axis=-1, keepdims=True) + NORM_EPS) * sub_ref[...] * diff_scale
            y = y + jnp.dot(o.astype(BF16), wa_ref[sl, :], preferred_element_type=F32)
    else:
        a_ref, b_ref, wa_ref, wb_ref, x_ref, o_ref = refs
        y = x_ref[...] + jnp.dot(a_ref[...].astype(BF16), wa_ref[...], preferred_element_type=F32)
        y = y + jnp.dot(b_ref[...].astype(BF16), wb_ref[...], preferred_element_type=F32)
    o_ref[...] = y


def _outproj(a, b, w, x, diff=None, name="outproj"):
    n, d = x.shape
    ka, kb = (a[0].shape[1] if diff else a.shape[1]), b.shape[1]
    tm = _row_tile(n, 512)
    row = lambda i: (i, 0)
    const = lambda i: (0, 0)
    wa, wb = w[:ka].astype(BF16), w[ka:].astype(BF16)
    if diff:
        lam, subln, scale, heads = diff
        args = [a[0], a[1], lam.reshape(1, 1).astype(F32), subln.reshape(1, -1).astype(F32), b, wa, wb, x]
        in_specs = [pl.BlockSpec((tm, ka), row), pl.BlockSpec((tm, ka), row),
                    pl.BlockSpec(memory_space=pltpu.SMEM), pl.BlockSpec((1, ka // heads), const),
                    pl.BlockSpec((tm, kb), row), pl.BlockSpec((ka, d), const), pl.BlockSpec((kb, d), const),
                    pl.BlockSpec((tm, d), row)]
        kern = functools.partial(_outproj_kernel, diff_heads=heads, diff_scale=scale)
    else:
        args = [a, b, wa, wb, x]
        in_specs = [pl.BlockSpec((tm, ka), row), pl.BlockSpec((tm, kb), row), pl.BlockSpec((ka, d), const),
                    pl.BlockSpec((kb, d), const), pl.BlockSpec((tm, d), row)]
        kern = functools.partial(_outproj_kernel, diff_heads=0, diff_scale=1.0)
    return pl.pallas_call(kern, out_shape=jax.ShapeDtypeStruct((n, d), F32), grid=(n // tm,), in_specs=in_specs,
                          out_specs=pl.BlockSpec((tm, d), row), compiler_params=_cparams(("arbitrary",)),
                          name=name)(*args)


def _qk(q, k):
    return lax.dot_general(q, k, (((1,), (1,)), ((), ())), preferred_element_type=F32)


def _softplus(z):
    return jnp.maximum(z, 0.0) + jnp.log(1.0 + jnp.exp(-jnp.abs(z)))


def _split_bf16(x):
    hi = x.astype(BF16)
    return hi, (x - hi.astype(F32)).astype(BF16)


def _lanes(x, width):
    if width <= LANES:
        return x[:, :width]
    return jnp.concatenate([x] * (width // LANES), axis=1)


def _attn_softmax_kernel(q_ref, k_ref, v_ref, o_ref, m_scr, l_scr, acc_scr, *, tq):
    i = pl.program_id(1)
    g, dv = q_ref.shape[1], v_ref.shape[-1]
    r = g * tq
    q = q_ref[0].reshape(r, q_ref.shape[-1])
    m_scr[...] = jnp.full(m_scr.shape, -jnp.inf, F32)
    l_scr[...] = jnp.zeros(l_scr.shape, F32)
    acc_scr[...] = jnp.zeros(acc_scr.shape, F32)

    def chunk(j, masked):
        start = pl.multiple_of(j * tq, tq)
        s = _qk(q, k_ref[0, pl.ds(start, tq), :])
        if masked:
            row = lax.broadcasted_iota(jnp.int32, (r, tq), 0) & (tq - 1)
            col = lax.broadcasted_iota(jnp.int32, (r, tq), 1)
            s = jnp.where(col <= row, s, -jnp.inf)
        m_prev = m_scr[...]
        m_new = jnp.maximum(m_prev, jnp.max(s, axis=1, keepdims=True))
        p = jnp.exp(s - _lanes(m_new, tq))
        alpha = jnp.exp(m_prev - m_new)
        l_scr[...] = alpha * l_scr[...] + jnp.sum(p, axis=1, keepdims=True)
        acc_scr[...] = _lanes(alpha, dv) * acc_scr[...] + jnp.dot(p.astype(BF16), v_ref[0, pl.ds(start, tq), :],
                                                                  preferred_element_type=F32)
        m_scr[...] = m_new

    def body(j, c):
        chunk(j, False)
        return c

    lax.fori_loop(0, i, body, 0)
    chunk(i, True)
    o_ref[0] = (acc_scr[...] / _lanes(l_scr[...], dv)).reshape(g, tq, dv)


def _attn_sb_kernel(q_ref, k_ref, v_ref, tri_ref, o_ref, c_scr, acc_scr, *, tq):
    i = pl.program_id(1)
    g, dv = q_ref.shape[1], v_ref.shape[-1]
    r = g * tq
    q = q_ref[0].reshape(r, q_ref.shape[-1])
    c_scr[...] = jnp.zeros(c_scr.shape, F32)
    acc_scr[...] = jnp.zeros(acc_scr.shape, F32)

    def chunk(j, masked):
        start = pl.multiple_of(j * tq, tq)
        z = _qk(q, k_ref[0, pl.ds(start, tq), :])
        sp = _softplus(z)
        if masked:
            row = lax.broadcasted_iota(jnp.int32, (r, tq), 0) & (tq - 1)
            col = lax.broadcasted_iota(jnp.int32, (r, tq), 1)
            mask = col < row
            sp = jnp.where(mask, sp, 0.0)
        si = jnp.dot(sp.astype(BF16), tri_ref[...], preferred_element_type=F32)
        c_prev = c_scr[...]
        w = jnp.exp(z - si - _lanes(c_prev, tq))
        if masked:
            w = jnp.where(mask, w, 0.0)
        acc_scr[...] += jnp.dot(w.astype(BF16), v_ref[0, pl.ds(start, tq), :], preferred_element_type=F32)
        c_scr[...] = c_prev + jnp.sum(sp, axis=1, keepdims=True)

    chunk(i, True)

    def body(jj, c):
        chunk(i - 1 - jj, False)
        return c

    lax.fori_loop(0, i, body, 0)
    o_ref[0] = acc_scr[...].reshape(g, tq, dv)


def _suffix_ones(n):
    return (lax.broadcasted_iota(jnp.int32, (n, n), 0) >= lax.broadcasted_iota(jnp.int32, (n, n), 1)).astype(BF16)


def _causal_attention(q, k, v, mode, tq, name):
    bh, g, t, dk = q.shape
    dv = v.shape[-1]
    kdiv, vdiv = bh // k.shape[0], bh // v.shape[0]
    tq = min(tq, t)
    assert t % tq == 0 and tq & (tq - 1) == 0
    r = g * tq
    in_specs = [pl.BlockSpec((1, g, tq, dk), lambda b, i: (b, 0, i, 0)),
                pl.BlockSpec((1, t, dk), lambda b, i: (b // kdiv, 0, 0)),
                pl.BlockSpec((1, t, dv), lambda b, i: (b // vdiv, 0, 0))]
    args = [q, k, v]
    if mode == "sb":
        in_specs.append(pl.BlockSpec((tq, tq), lambda b, i: (0, 0)))
        args.append(_suffix_ones(tq))
        kern = functools.partial(_attn_sb_kernel, tq=tq)
        scratch = [pltpu.VMEM((r, LANES), F32), pltpu.VMEM((r, dv), F32)]
    else:
        kern = functools.partial(_attn_softmax_kernel, tq=tq)
        scratch = [pltpu.VMEM((r, LANES), F32), pltpu.VMEM((r, LANES), F32), pltpu.VMEM((r, dv), F32)]
    return pl.pallas_call(kern, out_shape=jax.ShapeDtypeStruct((bh, g, t, dv), F32), grid=(bh, t // tq),
                          in_specs=in_specs, out_specs=pl.BlockSpec((1, g, tq, dv), lambda b, i: (b, 0, i, 0)),
                          scratch_shapes=scratch, compiler_params=_cparams(("arbitrary", "arbitrary")),
                          name=name)(*args)


def _pool_specs(npages, tail, layer, nchunks, nseq, u, reverse=False):
    specs = []
    zeros = (0,) * len(tail)
    for p in range(npages):
        if reverse:
            imap = lambda b, c, pt, p=p: (layer, pt[b * nseq + u, (nchunks - 1 - c) * npages + p]) + zeros
        else:
            imap = lambda b, c, pt, p=p: (layer, pt[b * nseq + u, c * npages + p]) + zeros
        specs.append(pl.BlockSpec((None, None) + tail, imap))
    return specs


def _softmax_init(m_scr, l_scr, acc_scr):
    m_scr[...] = jnp.full(m_scr.shape, -jnp.inf, F32)
    l_scr[...] = jnp.zeros(l_scr.shape, F32)
    acc_scr[...] = jnp.zeros(acc_scr.shape, F32)


def _softmax_update(s, v, m_scr, l_scr, acc_scr):
    m_prev = m_scr[...]
    m_new = jnp.maximum(m_prev, jnp.max(s, axis=1, keepdims=True))
    p = jnp.exp(s - m_new)
    alpha = jnp.exp(m_prev - m_new)
    l_scr[...] = alpha * l_scr[...] + jnp.sum(p, axis=1, keepdims=True)
    acc_scr[...] = alpha * acc_scr[...] + jnp.dot(p.astype(BF16), v, preferred_element_type=F32)
    m_scr[...] = m_new


def _softmax_finish(s_new, v_new, o_ref, m_scr, l_scr, acc_scr):
    m_prev = m_scr[...]
    m_fin = jnp.maximum(m_prev, s_new)
    p_new = jnp.exp(s_new - m_fin)
    alpha = jnp.exp(m_prev - m_fin)
    o_ref[...] = (alpha * acc_scr[...] + p_new * v_new) / (alpha * l_scr[...] + p_new)


def _rows(lo, n):
    return slice(lo * n, (lo + 1) * n)


def _decode_kernel(pt_ref, *refs, seq_fn, npages, nseq, nrow, npool, nextra):
    del pt_ref
    rows, pos = refs[:nrow], nrow
    pages = [[refs[pos + (a * nseq + u) * npages: pos + (a * nseq + u + 1) * npages] for a in range(npool)]
             for u in range(nseq)]
    pos += npool * nseq * npages
    extra, o_ref, scratch = refs[pos:pos + nextra], refs[pos + nextra], refs[pos + nextra + 1:]
    c = pl.program_id(1)
    for u in range(nseq):
        seq_fn(c, c == pl.num_programs(1) - 1, [x.at[u] for x in rows], pages[u], extra, o_ref.at[u],
               [x.at[u] for x in scratch])


def _mla_decode_seq(c, last, rows, pages, extra, o_ref, scratch):
    qa_ref, qp_ref, cnew_ref, pnew_ref = rows
    ckv_pages, kpe_pages = pages
    cb, pb, m_scr, l_scr, acc_scr = scratch
    pl.when(c == 0)(functools.partial(_softmax_init, m_scr, l_scr, acc_scr))
    for p in range(len(ckv_pages)):
        cb[_rows(p, PAGE_SIZE), :] = ckv_pages[p][...].astype(BF16)
        pb[:, _rows(p, PAGE_SIZE)] = kpe_pages[p][...].astype(BF16)
    s = _qk(qa_ref[...], cb[...]) + jnp.dot(qp_ref[...], pb[...], preferred_element_type=F32)
    _softmax_update(s, cb[...], m_scr, l_scr, acc_scr)

    @pl.when(last)
    def _():
        s_new = (jnp.sum(qa_ref[...].astype(F32) * cnew_ref[...], axis=1, keepdims=True)
                 + jnp.sum(qp_ref[...].astype(F32) * pnew_ref[...], axis=1, keepdims=True))
        _softmax_finish(s_new, cnew_ref[...], o_ref, m_scr, l_scr, acc_scr)


def _diff_decode_seq(c, last, rows, pages, extra, o_ref, scratch, *, kv_heads):
    q_ref, knew_ref, vnew_ref = rows
    k_pages, v_pages = pages
    kb, vb, m_scr, l_scr, acc_scr = scratch
    pl.when(c == 0)(functools.partial(_softmax_init, m_scr, l_scr, acc_scr))
    ve = vb.shape[1] // kv_heads
    for p in range(len(k_pages)):
        kb[:, _rows(p, PAGE_SIZE)] = k_pages[p][...].astype(BF16)
        for h in range(kv_heads):
            vb[_rows(p, PAGE_SIZE), _rows(h, ve)] = v_pages[p][pl.ds(h, PAGE_SIZE, stride=kv_heads), :].astype(BF16)
    s = jnp.dot(q_ref[...], kb[...], preferred_element_type=F32)
    _softmax_update(s, vb[...], m_scr, l_scr, acc_scr)

    @pl.when(last)
    def _():
        s_new = jnp.sum(q_ref[...].astype(F32) * knew_ref[...], axis=1, keepdims=True)
        _softmax_finish(s_new, vnew_ref[...], o_ref, m_scr, l_scr, acc_scr)


def _sb_decode_seq(c, last, rows, pages, extra, o_ref, scratch):
    (q_ref,), (tri_ref,) = rows, extra
    k_pages, v_pages = pages
    kb, vb, c_scr, acc_scr = scratch
    npages = len(k_pages)
    r = q_ref.shape[0]

    @pl.when(c == 0)
    def _():
        c_scr[...] = jnp.zeros(c_scr.shape, F32)
        acc_scr[...] = jnp.zeros(acc_scr.shape, F32)

    for p in range(npages):
        kb[:, _rows(p, PAGE_SIZE)] = k_pages[p][...].astype(BF16)
        vb[:, _rows(p, PAGE_SIZE)] = v_pages[p][...].astype(BF16)
    z = jnp.dot(q_ref[...], kb[...], preferred_element_type=F32)
    z_st = jnp.concatenate([z[:, _rows(p, PAGE_SIZE)] for p in range(npages)], axis=0)
    sit = jnp.dot(_softplus(z_st).astype(BF16), tri_ref[...], preferred_element_type=F32)
    si, tot = sit[:, :PAGE_SIZE], sit[:, PAGE_SIZE:]
    run = c_scr[...]
    later = [None] * npages
    for p in range(npages - 1, -1, -1):
        later[p] = run
        run = run + tot[_rows(p, r)]
    c_scr[...] = run
    w_st = jnp.exp(z_st - si - jnp.concatenate(later, axis=0))
    w = jnp.concatenate([w_st[_rows(p, r)] for p in range(npages)], axis=1)
    acc_scr[...] += _qk(w.astype(BF16), vb[...])

    @pl.when(last)
    def _():
        o_ref[...] = acc_scr[...]


def _decode_call(seq_fn, page_table, row_args, pools, extra, r, dv, scratch, name):
    b, n_pages = page_table.shape
    npages = math.gcd(DECODE_PAGES_PER_STEP, n_pages)
    nseq = math.gcd(DECODE_SEQS_PER_STEP, b)
    nchunks = n_pages // npages
    args = list(row_args)
    in_specs = [pl.BlockSpec((nseq,) + a.shape[1:], lambda i, c, pt: (i, 0, 0)) for a in row_args]
    for arr, tail, layer, reverse in pools:
        args += [arr] * (nseq * npages)
        for u in range(nseq):
            in_specs += _pool_specs(npages, tail, layer, nchunks, nseq, u, reverse)
    for arr in extra:
        args.append(arr)
        in_specs.append(pl.BlockSpec(arr.shape, lambda i, c, pt: (0, 0)))
    grid_spec = pltpu.PrefetchScalarGridSpec(
        num_scalar_prefetch=1, grid=(b // nseq, nchunks), in_specs=in_specs,
        out_specs=pl.BlockSpec((nseq, r, dv), lambda i, c, pt: (i, 0, 0)),
        scratch_shapes=[pltpu.VMEM((nseq,) + shape, dt) for shape, dt in scratch(npages * PAGE_SIZE)])
    kern = functools.partial(_decode_kernel, seq_fn=seq_fn, npages=npages, nseq=nseq, nrow=len(row_args),
                             npool=len(pools), nextra=len(extra))
    return pl.pallas_call(kern, out_shape=jax.ShapeDtypeStruct((b, r, dv), F32), grid_spec=grid_spec,
                          compiler_params=_cparams(("arbitrary", "arbitrary")), name=name)(page_table, *args)


def _softmax_scratch(r, dv):
    return [((r, 1), F32), ((r, 1), F32), ((r, dv), F32)]


def _headwise_kernel(x_ref, w_ref, o_ref):
    h, din, dout = w_ref.shape
    for i in range(h):
        o_ref[:, i * dout:(i + 1) * dout] = jnp.dot(x_ref[:, i * din:(i + 1) * din].astype(BF16), w_ref[i],
                                                    preferred_element_type=F32)


def _headwise(x, w, name):
    n = x.shape[0]
    h, din, dout = w.shape
    return pl.pallas_call(_headwise_kernel, out_shape=jax.ShapeDtypeStruct((n, h * dout), F32),
                          name=name)(x, w.astype(BF16))


def _moe_kernel(x_ref, g_ref, wrh_ref, wrl_ref, br_ref, wg_ref, wu_ref, wd_ref, fg_ref, o_ref,
                xn_scr, comb_scr, acc_scr, *, final_norm):
    e = pl.program_id(1)
    tm = x_ref.shape[0]
    lane = lax.broadcasted_iota(jnp.int32, (tm, LANES), 1)

    @pl.when(e == 0)
    def _():
        x = x_ref[...]
        xn = x * lax.rsqrt(jnp.mean(x * x, axis=-1, keepdims=True) + NORM_EPS) * g_ref[...]
        xh, xl = _split_bf16(xn)
        xn_scr[...] = xh
        wrh = wrh_ref[...]
        lg = (jnp.dot(xh, wrh, preferred_element_type=F32) + jnp.dot(xl, wrh, preferred_element_type=F32)
              + jnp.dot(xh, wrl_ref[...], preferred_element_type=F32) + br_ref[...])
        big = jnp.int32(LANES)
        is_g = jnp.logical_and(lane >= N_EXPERTS, lane < N_EXPERTS + N_GROUPS)
        gl = jnp.where(is_g, lg, -jnp.inf)
        gmax = jnp.max(gl, axis=1, keepdims=True)
        gw = 1.0 / jnp.sum(jnp.exp(gl - gmax), axis=1, keepdims=True)
        gsel = jnp.min(jnp.where(gl == gmax, lane, big), axis=1, keepdims=True) - N_EXPERTS
        in_grp = jnp.logical_and(lane < N_EXPERTS, (lane >> 3) == gsel)
        el = jnp.where(in_grp, lg, -jnp.inf)
        v1 = jnp.max(el, axis=1, keepdims=True)
        i1 = jnp.min(jnp.where(el == v1, lane, big), axis=1, keepdims=True)
        el2 = jnp.where(lane == i1, -jnp.inf, el)
        v2 = jnp.max(el2, axis=1, keepdims=True)
        i2 = jnp.min(jnp.where(el2 == v2, lane, big), axis=1, keepdims=True)
        e2 = jnp.exp(v2 - v1)
        w1 = gw / (1.0 + e2)
        comb_scr[...] = jnp.where(lane == i1, w1, 0.0) + jnp.where(lane == i2, w1 * e2, 0.0)
        acc_scr[...] = jnp.zeros(acc_scr.shape, F32)

    xb = xn_scr[...]
    h = jnp.dot(xb, wg_ref[0], preferred_element_type=F32)
    u = jnp.dot(xb, wu_ref[0], preferred_element_type=F32)
    ce = jnp.sum(jnp.where(lane == e, comb_scr[...], 0.0), axis=1, keepdims=True)
    act = h * (1.0 / (1.0 + jnp.exp(-h))) * u * ce
    acc_scr[...] += jnp.dot(act.astype(BF16), wd_ref[0], preferred_element_type=F32)

    @pl.when(e == pl.num_programs(1) - 1)
    def _():
        y = x_ref[...] + acc_scr[...]
        if final_norm:
            y = y * lax.rsqrt(jnp.mean(y * y, axis=-1, keepdims=True) + NORM_EPS) * fg_ref[...]
        o_ref[...] = y


def _moe(x, norm_g, w_group, b_group, w_router, b_router, w_gate, w_up, w_down, final_g, name):
    n, d = x.shape
    ne, _, f = w_gate.shape
    tm = _row_tile(n, 1024)
    wr = jnp.zeros((d, LANES), F32)
    wr = wr.at[:, :N_EXPERTS].set(w_router.reshape(d, N_EXPERTS)).at[:, N_EXPERTS:N_EXPERTS + N_GROUPS].set(w_group)
    br = jnp.zeros((1, LANES), F32)
    br = br.at[0, :N_EXPERTS].set(b_router.reshape(N_EXPERTS)).at[0, N_EXPERTS:N_EXPERTS + N_GROUPS].set(b_group)
    wrh = wr.astype(BF16)
    wrl = (wr - wrh.astype(F32)).astype(BF16)
    fg = jnp.ones((1, d), F32) if final_g is None else final_g.reshape(1, d).astype(F32)
    row = lambda i, e: (i, 0)
    const = lambda i, e: (0, 0)
    in_specs = [pl.BlockSpec((tm, d), row), pl.BlockSpec((1, d), const), pl.BlockSpec((d, LANES), const),
                pl.BlockSpec((d, LANES), const), pl.BlockSpec((1, LANES), const),
                pl.BlockSpec((1, d, f), lambda i, e: (e, 0, 0)), pl.BlockSpec((1, d, f), lambda i, e: (e, 0, 0)),
                pl.BlockSpec((1, f, d), lambda i, e: (e, 0, 0)), pl.BlockSpec((1, d), const)]
    return pl.pallas_call(
        functools.partial(_moe_kernel, final_norm=final_g is not None),
        out_shape=jax.ShapeDtypeStruct((n, d), F32), grid=(n // tm, ne), in_specs=in_specs,
        out_specs=pl.BlockSpec((tm, d), row),
        scratch_shapes=[pltpu.VMEM((tm, d), BF16), pltpu.VMEM((tm, LANES), F32), pltpu.VMEM((tm, d), F32)],
        compiler_params=_cparams(("arbitrary", "arbitrary")), name=name,
    )(x, norm_g.reshape(1, d).astype(F32), wrh, wrl, br, w_gate.astype(BF16), w_up.astype(BF16),
      w_down.astype(BF16), fg)


def _rwkv_slot(s, idx, lane, r_ref, w_ref, k_ref, kk_ref, kka_ref, vt_ref, y_scr, c):
    row = lambda ref: ref[c, pl.ds(idx, 1), :]
    sel = lane == idx
    vcol = jnp.sum(jnp.where(sel, vt_ref[c], 0.0), axis=1, keepdims=True)
    sa = -jnp.sum(s * row(kk_ref), axis=1, keepdims=True)
    s = s * row(w_ref) + sa * row(kka_ref) + vcol * row(k_ref)
    y = jnp.sum(s * row(r_ref), axis=1, keepdims=True)
    y_scr[c] = jnp.where(sel, y, y_scr[c])
    return s


def _rwkv_scan_kernel(r_ref, w_ref, k_ref, kk_ref, kka_ref, vt_ref, yt_ref, sfin_ref, s_scr, y_scr, wrb_scr,
                      gamma_scr):
    tblk = pl.program_id(0)
    nch, tb, hs = r_ref.shape

    @pl.when(tblk == 0)
    def _():
        s_scr[...] = jnp.zeros(s_scr.shape, F32)

    lane = lax.broadcasted_iota(jnp.int32, (1, hs, tb), 2)
    r_blk = r_ref[...]
    beta = jnp.sum(kka_ref[...] * r_blk, axis=2, keepdims=True)
    wrb_scr[...] = w_ref[...] * r_blk - beta * kk_ref[...]
    gamma_scr[...] = jnp.broadcast_to(jnp.sum(k_ref[...] * r_blk, axis=2, keepdims=True), gamma_scr.shape)
    pick_row = lax.broadcasted_iota(jnp.int32, (tb, 2 * tb), 0)
    pick_col = lax.broadcasted_iota(jnp.int32, (tb, 2 * tb), 1)

    def step(t, vcol):
        row = lambda ref: ref[:, pl.ds(t, 1), :]
        s = s_scr[...]
        sa = -jnp.sum(s * row(kk_ref), axis=2, keepdims=True)
        y0 = jnp.sum(s * row(wrb_scr), axis=2, keepdims=True)
        s_scr[...] = s * row(w_ref) + sa * row(kka_ref) + vcol[:, :, :hs] * row(k_ref)
        pltpu.store(y_scr, y0 + vcol * row(gamma_scr), mask=jnp.broadcast_to(lane == t, y_scr.shape))

    def pair(p, carry):
        t0 = 2 * p
        pick = jnp.where(pick_row == t0 + (pick_col >= tb).astype(jnp.int32), 1.0, 0.0).astype(BF16)
        vb = jnp.dot(vt_ref[...].reshape(nch * hs, tb), pick, preferred_element_type=F32).reshape(nch, hs, 2 * tb)
        step(t0, vb[:, :, :tb])
        step(t0 + 1, vb[:, :, tb:])
        return carry

    lax.fori_loop(0, tb // 2, pair, 0)
    yt_ref[...] = y_scr[...]

    @pl.when(tblk == pl.num_programs(0) - 1)
    def _():
        sfin_ref[...] = s_scr[...]


def _rwkv_scan(r, w, k, kk, kka, vt):
    nch, t, hs = r.shape
    tb = min(t, LANES)
    assert t % tb == 0
    rows = pl.BlockSpec((nch, tb, hs), lambda i: (0, i, 0))
    cols = pl.BlockSpec((nch, hs, tb), lambda i: (0, 0, i))
    return pl.pallas_call(
        _rwkv_scan_kernel,
        out_shape=[jax.ShapeDtypeStruct((nch, hs, t), F32), jax.ShapeDtypeStruct((nch, hs, hs), F32)],
        grid=(t // tb,), in_specs=[rows] * 5 + [cols],
        out_specs=[cols, pl.BlockSpec((nch, hs, hs), lambda i: (0, 0, 0))],
        scratch_shapes=[pltpu.VMEM((nch, hs, hs), F32), pltpu.VMEM((nch, hs, tb), F32),
                        pltpu.VMEM((nch, tb, hs), F32), pltpu.VMEM((nch, tb, tb), F32)],
        compiler_params=_cparams(("arbitrary",)), name="rwkv_scan")(r, w, k, kk, kka, vt)


def _rwkv_step_kernel(s_ref, r_ref, w_ref, k_ref, kk_ref, kka_ref, vt_ref, so_ref, yt_ref, y_scr):
    nb, hs = r_ref.shape[1], r_ref.shape[2]
    lane = lax.broadcasted_iota(jnp.int32, (hs, nb), 1)
    y_scr[...] = jnp.zeros(y_scr.shape, F32)

    def step(b, carry):
        so_ref[b, 0] = _rwkv_slot(s_ref[b, 0], b, lane, r_ref, w_ref, k_ref, kk_ref, kka_ref, vt_ref, y_scr, 0)
        return carry

    lax.fori_loop(0, nb, step, 0)
    yt_ref[...] = y_scr[...]


def _rwkv_step(s, r, w, k, kk, kka, vt):
    b, h, hs, _ = s.shape
    nb = min(b, LANES)
    assert b % nb == 0
    sspec = pl.BlockSpec((nb, 1, hs, hs), lambda j, i: (i, j, 0, 0))
    rows = pl.BlockSpec((1, nb, hs), lambda j, i: (j, i, 0))
    cols = pl.BlockSpec((1, hs, nb), lambda j, i: (j, 0, i))
    return pl.pallas_call(
        _rwkv_step_kernel,
        out_shape=[jax.ShapeDtypeStruct(s.shape, F32), jax.ShapeDtypeStruct((h, hs, b), F32)],
        grid=(h, b // nb), in_specs=[sspec] + [rows] * 5 + [cols], out_specs=[sspec, cols],
        scratch_shapes=[pltpu.VMEM((1, hs, nb), F32)],
        compiler_params=_cparams(("arbitrary", "arbitrary")), name="rwkv_step")(s, r, w, k, kk, kka, vt)


def _rope_tables(pos, reps):
    half = MLA_ROPE // 2
    inv = ROPE_BASE ** (-jnp.arange(half, dtype=F32) / half)
    ang = pos.astype(F32)[:, None] * inv[None, :]
    cos, sin = jnp.cos(ang), jnp.sin(ang)
    return jnp.tile(jnp.concatenate([cos, cos], -1), (1, reps)), jnp.tile(jnp.concatenate([-sin, sin], -1), (1, reps))


def _swap_halves(w, width):
    k, m = w.shape
    w = w.reshape(k, m // width, 2, width // 2)
    return jnp.concatenate([w[:, :, 1], w[:, :, 0]], axis=2).reshape(k, m)


def _even_weights(w_in, q_norm, w_uq, kv_norm, w_uk, w_uv, w_out):
    kpe_lo = MLA_Q_LORA + MLA_KV_LORA
    w_in_ext = jnp.concatenate([w_in, _swap_halves(w_in[:, kpe_lo:kpe_lo + MLA_ROPE], MLA_ROPE)], axis=1)
    wq = w_uq.reshape(MLA_Q_LORA, MLA_HEADS, MLA_NOPE + MLA_ROPE)
    wq_n = wq[:, :, :MLA_NOPE].reshape(MLA_Q_LORA, -1)
    wq_p = wq[:, :, MLA_NOPE:].reshape(MLA_Q_LORA, -1)
    wq_ext = jnp.concatenate([wq_n, wq_p, _swap_halves(wq_p, MLA_ROPE)], axis=1)
    w_kv = jnp.concatenate([w_uk.reshape(MLA_KV_LORA, -1), w_uv.reshape(MLA_KV_LORA, -1)], axis=1)
    return dict(w_in=w_in_ext, q_norm=q_norm, wq=wq_ext, kv_norm=kv_norm, w_kv=w_kv, w_uk=w_uk, w_uv=w_uv,
                w_out=w_out)


def _even_project(h_in, attn_g, wts, pos, b, t):
    n = h_in.shape[0]
    proj = _linear(h_in, wts["w_in"], g=attn_g, name="even_in")
    o = 0
    parts = []
    for wdt in (MLA_Q_LORA, MLA_KV_LORA, MLA_ROPE, SB_HEADS * SB_HD, SB_KV_HEADS * SB_HD, SB_KV_HEADS * SB_HD,
                MLA_ROPE):
        parts.append(proj[:, o:o + wdt])
        o += wdt
    q_lat, kv_lat, kpe_raw, sq, sk, sv, kpe_sw = parts
    cos1, sin1 = _rope_tables(pos, 1)
    cos1, sin1 = jnp.tile(cos1, (b, 1)), jnp.tile(sin1, (b, 1))
    kpe = kpe_raw * cos1 + kpe_sw * sin1
    qall = _linear(q_lat, wts["wq"], g=wts["q_norm"], name="mla_q_up")
    hp = MLA_HEADS * MLA_ROPE
    q_nope, q_pe, q_pe_sw = qall[:, :MLA_HEADS * MLA_NOPE], qall[:, -2 * hp:-hp], qall[:, -hp:]
    q_pe = q_pe * jnp.tile(cos1, (1, MLA_HEADS)) + q_pe_sw * jnp.tile(sin1, (1, MLA_HEADS))
    return q_nope, q_pe, kv_lat, kpe, sq, sk, sv


def _heads_first(x, b, t, h):
    d = x.shape[1] // h
    return x.reshape(b, t, h, d).transpose(0, 2, 1, 3).reshape(b * h, t, d)


def _even_prompt(x, attn_g, wts, b, t):
    pos = jnp.arange(t, dtype=jnp.int32)
    q_nope, q_pe, kv_lat, kpe, sq, sk, sv = _even_project(x, attn_g, wts, pos, b, t)
    kexp, ckv = _linear(kv_lat, wts["w_kv"], g=wts["kv_norm"], emit_normed=True, name="mla_kv_up")
    hn = MLA_HEADS * MLA_NOPE
    scale = (MLA_NOPE + MLA_ROPE) ** -0.5
    q = jnp.concatenate([_heads_first(q_nope, b, t, MLA_HEADS), _heads_first(q_pe, b, t, MLA_HEADS)], axis=-1)
    q = (q * scale).astype(BF16)[:, None]
    k = jnp.concatenate([_heads_first(kexp[:, :hn], b, t, MLA_HEADS),
                         jnp.broadcast_to(kpe.reshape(b, 1, t, MLA_ROPE), (b, MLA_HEADS, t, MLA_ROPE)
                                          ).reshape(b * MLA_HEADS, t, MLA_ROPE)], axis=-1).astype(BF16)
    v = _heads_first(kexp[:, hn:], b, t, MLA_HEADS).astype(BF16)
    mla = _causal_attention(q, k, v, "softmax", 512, "mla_prompt")
    mla = mla.reshape(b, MLA_HEADS, t, MLA_V).transpose(0, 2, 1, 3).reshape(b * t, MLA_HEADS * MLA_V)
    grp = SB_HEADS // SB_KV_HEADS
    sq_h = (sq * SB_HD ** -0.5).astype(BF16).reshape(b, t, SB_KV_HEADS, grp, SB_HD).transpose(0, 2, 3, 1, 4)
    sq_h = sq_h.reshape(b * SB_KV_HEADS, grp, t, SB_HD)
    sbo = _causal_attention(sq_h, _heads_first(sk, b, t, SB_KV_HEADS).astype(BF16),
                            _heads_first(sv, b, t, SB_KV_HEADS).astype(BF16), "sb", 256, "sb_prompt")
    sbo = sbo.reshape(b, SB_KV_HEADS, grp, t, SB_HD).transpose(0, 3, 1, 2, 4).reshape(b * t, SB_HEADS * SB_HD)
    y = _outproj(mla, sbo, wts["w_out"], x, name="even_out")
    return y, (ckv, kpe, sk, sv)


def _feature_major(cache):
    nl, npool, page = cache.shape[:3]
    nd = cache.ndim
    return cache.transpose((0, 1) + tuple(range(3, nd)) + (2,)).reshape(nl, npool, -1, page)


def _even_sample(x, attn_g, wts, page_table, layer, c_ckv, c_kpe, c_sk, c_sv):
    b = x.shape[0]
    past_len = page_table.shape[1] * PAGE_SIZE
    pos = jnp.full((1,), past_len, jnp.int32)
    q_nope, q_pe, kv_lat, kpe, sq, sk, sv = _even_project(x, attn_g, wts, pos, b, 1)
    _, ckv = _linear(kv_lat, wts["w_kv"], g=wts["kv_norm"], emit_normed=True, name="mla_kv_norm_s")
    scale = (MLA_NOPE + MLA_ROPE) ** -0.5
    q_abs = _headwise(q_nope, wts["w_uk"].transpose(1, 2, 0), "mla_q_abs")
    q_abs = (q_abs * scale).astype(BF16).reshape(b, MLA_HEADS, MLA_KV_LORA)
    q_pe = (q_pe * scale).astype(BF16).reshape(b, MLA_HEADS, MLA_ROPE)
    lat = _decode_call(
        _mla_decode_seq, page_table, [q_abs, q_pe, ckv[:, None], kpe[:, None]],
        [(c_ckv, (PAGE_SIZE, MLA_KV_LORA), layer, False), (_feature_major(c_kpe), (MLA_ROPE, PAGE_SIZE), layer, False)],
        [], MLA_HEADS, MLA_KV_LORA,
        lambda tk: [((tk, MLA_KV_LORA), BF16), ((MLA_ROPE, tk), BF16)] + _softmax_scratch(MLA_HEADS, MLA_KV_LORA),
        "mla_decode")
    mla = _headwise(lat.reshape(b, MLA_HEADS * MLA_KV_LORA), wts["w_uv"].transpose(1, 0, 2), "mla_v_up")
    kvw = SB_KV_HEADS * SB_HD
    grp = SB_HEADS // SB_KV_HEADS
    sq_rows = (sq * SB_HD ** -0.5).reshape(b, SB_KV_HEADS, grp, 1, SB_HD) * jnp.eye(SB_KV_HEADS, dtype=F32)[
        None, :, None, :, None]
    sq_rows = sq_rows.reshape(b, SB_HEADS, kvw).astype(BF16)
    tri = jnp.concatenate([_suffix_ones(PAGE_SIZE), jnp.ones((PAGE_SIZE, PAGE_SIZE), BF16)], axis=1)
    sbo = _decode_call(
        _sb_decode_seq, page_table, [sq_rows],
        [(_feature_major(c_sk), (kvw, PAGE_SIZE), layer, True), (_feature_major(c_sv), (kvw, PAGE_SIZE), layer, True)],
        [tri], SB_HEADS, kvw,
        lambda tk: [((kvw, tk), BF16), ((kvw, tk), BF16), ((SB_HEADS, LANES), F32), ((SB_HEADS, kvw), F32)],
        "sb_decode")
    sbo = sbo.reshape(b, SB_KV_HEADS, grp, SB_KV_HEADS, SB_HD)
    sbo = jnp.einsum("bkgjd,kj->bkgd", sbo, jnp.eye(SB_KV_HEADS, dtype=F32)).reshape(b, SB_HEADS * SB_HD)
    y = _outproj(mla, sbo, wts["w_out"], x, name="even_out_s")
    return y, (ckv, kpe, sk, sv)


def _rwkv_prepare(rw, prev, p):
    xs = rw + (prev - rw) * p["mu"]
    o = 0
    parts = []
    for wdt in (RW_DIM, RW_DIM, RW_DIM, RW_W_LORA, RW_A_LORA, RW_G_LORA):
        parts.append(xs[:, o:o + wdt])
        o += wdt
    r, k, v, xw, xa, xg = parts
    w_log = -jax.nn.softplus(-(p["w0"] + _linear(jnp.tanh(xw), p["w2"], name="rw_w_lora"))) - 0.5
    decay = jnp.exp(-jnp.exp(w_log))
    a = jax.nn.sigmoid(p["a0"] + _linear(xa, p["a2"], name="rw_a_lora"))
    g = _linear(jax.nn.sigmoid(xg), p["g2"], name="rw_g_lora")
    n = r.shape[0]
    kk = (k * p["kk"]).reshape(n, RW_HEADS, RW_HS)
    kk = (kk * lax.rsqrt(jnp.maximum(jnp.sum(kk * kk, axis=-1, keepdims=True), 1e-24))).reshape(n, RW_DIM)
    k = k * (1.0 + (a - 1.0) * p["ka"])
    return r, decay, k, v, kk, kk * a, g


def _rwkv_finish(yr, r, k, v, g, p):
    n = yr.shape[0]
    yh = yr.reshape(n, RW_HEADS, RW_HS)
    mean = jnp.mean(yh, axis=-1, keepdims=True)
    var = jnp.mean(jnp.square(yh - mean), axis=-1, keepdims=True)
    yn = ((yh - mean) * lax.rsqrt(var + RW_LN_EPS)).reshape(n, RW_DIM) * p["ln_w"] + p["ln_b"]
    rk = (r * k * p["rk"].reshape(1, RW_DIM)).reshape(n, RW_HEADS, RW_HS)
    bonus = (jnp.sum(rk, axis=-1, keepdims=True) * v.reshape(n, RW_HEADS, RW_HS)).reshape(n, RW_DIM)
    return (yn + bonus) * g


def _odd_split(proj):
    dq_w = DF_HEADS * 2 * DF_HD
    dkv_w = DF_KV_HEADS * 2 * DF_HD
    return proj[:, :dq_w], proj[:, dq_w:dq_w + dkv_w], proj[:, dq_w + dkv_w:dq_w + 2 * dkv_w], proj[:, dq_w + 2 * dkv_w:]


def _odd_prompt(x, attn_g, p, lam, lam_init, b, t):
    n = x.shape[0]
    proj = _linear(x, p["w_in"], g=attn_g, name="odd_in")
    dq, dk, dv, rw = _odd_split(proj)
    grp = DF_HEADS // DF_KV_HEADS
    q = (dq * DF_HD ** -0.5).astype(BF16).reshape(b, t, DF_KV_HEADS, grp, 2, DF_HD).transpose(0, 2, 4, 3, 1, 5)
    q = q.reshape(b * DF_KV_HEADS * 2, grp, t, DF_HD)
    k = dk.astype(BF16).reshape(b, t, DF_KV_HEADS * 2, DF_HD).transpose(0, 2, 1, 3).reshape(-1, t, DF_HD)
    v = _heads_first(dv, b, t, DF_KV_HEADS).astype(BF16)
    o = _causal_attention(q, k, v, "softmax", 512, "diff_prompt")
    o = o.reshape(b, DF_KV_HEADS, 2, grp, t, 2 * DF_HD).transpose(2, 0, 4, 1, 3, 5).reshape(2, n, DF_HEADS * 2 * DF_HD)
    rwb = rw.reshape(b, t, -1)
    prev = jnp.concatenate([jnp.zeros((b, 1, rwb.shape[-1]), F32), rwb[:, :-1]], axis=1).reshape(n, -1)
    r, decay, k_, v_, kk, kka, g = _rwkv_prepare(rw, prev, p)
    hf = lambda z: _heads_first(z, b, t, RW_HEADS)
    vt = hf(v_).transpose(0, 2, 1).astype(BF16)
    yt, s_fin = _rwkv_scan(hf(r), hf(decay), hf(k_), hf(kk), hf(kka), vt)
    yr = yt.reshape(b, RW_HEADS, RW_HS, t).transpose(0, 3, 1, 2).reshape(n, RW_DIM)
    rw_out = _rwkv_finish(yr, r, k_, v_, g, p)
    y = _outproj((o[0], o[1]), rw_out, p["w_out"], x, diff=(lam, p["subln"], 1.0 - lam_init, DF_HEADS),
                 name="odd_out")
    s_fin = s_fin.reshape(b, RW_HEADS, RW_HS, RW_HS)
    return y, (dk, dv), s_fin, rwb[:, -1]


def _odd_sample(x, attn_g, p, lam, lam_init, page_table, layer, c_dk, c_dv, shift_prev, s_prev):
    b = x.shape[0]
    proj = _linear(x, p["w_in"], g=attn_g, name="odd_in_s")
    dq, dk, dv, rw = _odd_split(proj)
    nl, npool = c_dk.shape[:2]
    kw = DF_KV_HEADS * 2 * DF_HD
    vw = DF_KV_HEADS * 2 * DF_HD
    q_rows = _block_rows_kgc(dq * DF_HD ** -0.5).astype(BF16)
    nrow = DF_HEADS * 2
    o = _decode_call(
        functools.partial(_diff_decode_seq, kv_heads=DF_KV_HEADS), page_table, [q_rows, dk[:, None], dv[:, None]],
        [(_feature_major(c_dk), (kw, PAGE_SIZE), layer, False),
         (c_dv.reshape(nl, npool, PAGE_SIZE * DF_KV_HEADS, 2 * DF_HD), (PAGE_SIZE * DF_KV_HEADS, 2 * DF_HD), layer,
          False)],
        [], nrow, vw,
        lambda tk: [((kw, tk), BF16), ((tk, vw), BF16)] + _softmax_scratch(nrow, vw),
        "diff_decode")
    grp = DF_HEADS // DF_KV_HEADS
    o = o.reshape(b, DF_KV_HEADS, grp, 2, DF_KV_HEADS, 2 * DF_HD)
    o = jnp.einsum("bkgcje,kj->cbkge", o, jnp.eye(DF_KV_HEADS, dtype=F32)).reshape(2, b, DF_HEADS * 2 * DF_HD)
    r, decay, k_, v_, kk, kka, g = _rwkv_prepare(rw, shift_prev, p)
    hb = lambda z: z.reshape(b, RW_HEADS, RW_HS).transpose(1, 0, 2)
    s_new, yt = _rwkv_step(s_prev, hb(r), hb(decay), hb(k_), hb(kk), hb(kka), hb(v_).transpose(0, 2, 1))
    yr = yt.transpose(2, 0, 1).reshape(b, RW_DIM)
    rw_out = _rwkv_finish(yr, r, k_, v_, g, p)
    y = _outproj((o[0], o[1]), rw_out, p["w_out"], x, diff=(lam, p["subln"], 1.0 - lam_init, DF_HEADS),
                 name="odd_out_s")
    return y, (dk, dv), s_new, rw


def _block_rows_kgc(dq):
    b = dq.shape[0]
    grp = DF_HEADS // DF_KV_HEADS
    q = dq.reshape(b, DF_KV_HEADS, grp, 2, 1, 1, DF_HD)
    eye_k = jnp.eye(DF_KV_HEADS, dtype=dq.dtype)[None, :, None, None, :, None, None]
    eye_c = jnp.eye(2, dtype=dq.dtype)[None, None, None, :, None, :, None]
    return (q * eye_k * eye_c).reshape(b, DF_KV_HEADS * grp * 2, DF_KV_HEADS * 2 * DF_HD)


def kernel(x_prompt, x_sample, cache_mla_ckv, cache_mla_kpe, cache_sb_k, cache_sb_v, cache_df_k, cache_df_v, state_rw_s, state_rw_shift, page_table, attn_norm, ffn_norm, final_norm, ev_w_in, mla_q_norm, mla_w_uq, mla_kv_norm, mla_w_uk, mla_w_uv, ev_w_out, od_w_in, df_lq1, df_lk1, df_lq2, df_lk2, df_subln, rw_mu, rw_w0, rw_w2, rw_a0, rw_a2, rw_g2, rw_kk, rw_ka, rw_rk, rw_ln_w, rw_ln_b, od_w_out, moe_w_group, moe_b_group, moe_w_router, moe_b_router, moe_w_gate, moe_w_up, moe_w_down):
    bp, tp, d = x_prompt.shape
    bs, ts, _ = x_sample.shape
    assert ts == 1
    depth = attn_norm.shape[0]
    xp = x_prompt.reshape(bp * tp, d)
    xs = x_sample.reshape(bs, d)
    outs = {k: [] for k in ("ckv_p", "ckv_s", "kpe_p", "kpe_s", "sbk_p", "sbk_s", "sbv_p", "sbv_s", "dfk_p", "dfk_s",
                            "dfv_p", "dfv_s", "rws_p", "rws_s", "rwsh_p", "rwsh_s")}
    for layer in range(depth):
        i = layer // 2
        if layer % 2 == 0:
            wts = _even_weights(ev_w_in[i], mla_q_norm[i], mla_w_uq[i], mla_kv_norm[i], mla_w_uk[i], mla_w_uv[i],
                                ev_w_out[i])
            xp, rp = _even_prompt(xp, attn_norm[layer], wts, bp, tp)
            xs, rs = _even_sample(xs, attn_norm[layer], wts, page_table, i, cache_mla_ckv, cache_mla_kpe,
                                  cache_sb_k, cache_sb_v)
            outs["ckv_p"].append(rp[0].reshape(bp, tp, MLA_KV_LORA))
            outs["kpe_p"].append(rp[1].reshape(bp, tp, MLA_ROPE))
            outs["sbk_p"].append(rp[2].reshape(bp, tp, SB_KV_HEADS, SB_HD))
            outs["sbv_p"].append(rp[3].reshape(bp, tp, SB_KV_HEADS, SB_HD))
            outs["ckv_s"].append(rs[0].reshape(bs, ts, MLA_KV_LORA))
            outs["kpe_s"].append(rs[1].reshape(bs, ts, MLA_ROPE))
            outs["sbk_s"].append(rs[2].reshape(bs, ts, SB_KV_HEADS, SB_HD))
            outs["sbv_s"].append(rs[3].reshape(bs, ts, SB_KV_HEADS, SB_HD))
        else:
            lam_init = 0.8 - 0.6 * math.exp(-0.3 * layer)
            lam = (jnp.exp(jnp.sum(df_lq1[i] * df_lk1[i])) - jnp.exp(jnp.sum(df_lq2[i] * df_lk2[i])) + lam_init)
            p = dict(w_in=od_w_in[i], subln=df_subln[i], mu=rw_mu[i], w0=rw_w0[i], w2=rw_w2[i], a0=rw_a0[i],
                     a2=rw_a2[i], g2=rw_g2[i], kk=rw_kk[i], ka=rw_ka[i], rk=rw_rk[i], ln_w=rw_ln_w[i],
                     ln_b=rw_ln_b[i], w_out=od_w_out[i])
            xp, kvp, sp, shp = _odd_prompt(xp, attn_norm[layer], p, lam, lam_init, bp, tp)
            xs, kvs, ss, shs = _odd_sample(xs, attn_norm[layer], p, lam, lam_init, page_table, i, cache_df_k,
                                           cache_df_v, state_rw_shift[i], state_rw_s[i])
            outs["dfk_p"].append(kvp[0].reshape(bp, tp, DF_KV_HEADS, 2, DF_HD))
            outs["dfv_p"].append(kvp[1].reshape(bp, tp, DF_KV_HEADS, 2 * DF_HD))
            outs["dfk_s"].append(kvs[0].reshape(bs, ts, DF_KV_HEADS, 2, DF_HD))
            outs["dfv_s"].append(kvs[1].reshape(bs, ts, DF_KV_HEADS, 2 * DF_HD))
            outs["rws_p"].append(sp)
            outs["rws_s"].append(ss)
            outs["rwsh_p"].append(shp)
            outs["rwsh_s"].append(shs)
        fin = final_norm if layer == depth - 1 else None
        mw = (moe_w_group[layer], moe_b_group[layer], moe_w_router[layer], moe_b_router[layer], moe_w_gate[layer],
              moe_w_up[layer], moe_w_down[layer])
        xp = _moe(xp, ffn_norm[layer], *mw, fin, "moe_p")
        xs = _moe(xs, ffn_norm[layer], *mw, fin, "moe_s")
    st = lambda key: jnp.stack(outs[key], axis=0)
    return (xp.reshape(bp, tp, d), xs.reshape(bs, ts, d), st("ckv_p"), st("ckv_s"), st("kpe_p"), st("kpe_s"),
            st("sbk_p"), st("sbk_s"), st("sbv_p"), st("sbv_s"), st("dfk_p"), st("dfk_s"), st("dfv_p"), st("dfv_s"),
            st("rws_p"), st("rws_s"), st("rwsh_p"), st("rwsh_s"))
```

```python
import functools
import math

import jax
import jax.numpy as jnp
from jax import lax
from jax.experimental import pallas as pl
from jax.experimental.pallas import tpu as pltpu

F32 = jnp.float32
BF16 = jnp.bfloat16

PAGE_SIZE = 128
NORM_EPS = 1e-6
MLA_HEADS, MLA_Q_LORA, MLA_KV_LORA, MLA_NOPE, MLA_ROPE, MLA_V = 8, 384, 256, 64, 32, 64
ROPE_BASE = 10000.0
SB_HEADS, SB_KV_HEADS, SB_HD = 8, 2, 64
DF_HEADS, DF_KV_HEADS, DF_HD = 4, 2, 64
RW_HEADS, RW_HS = 8, 64
RW_DIM = RW_HEADS * RW_HS
RW_W_LORA, RW_A_LORA, RW_G_LORA = 64, 64, 128
RW_LN_EPS = 64e-5
N_GROUPS, EXPERTS_PER_GROUP = 4, 8
N_EXPERTS = N_GROUPS * EXPERTS_PER_GROUP
LANES = 128
VMEM_LIMIT = 48 * 1024 * 1024
DECODE_PAGES_PER_STEP = 64
DECODE_SEQS_PER_STEP = 1
MOE_EXPERTS_PER_STEP = 4


def _cparams(sem):
    return pltpu.CompilerParams(dimension_semantics=sem, vmem_limit_bytes=VMEM_LIMIT)


def _row_tile(n, pref):
    t = min(n, pref)
    while n % t:
        t //= 2
    return t


def _linear_kernel(*refs, norm, has_res, emit_normed):
    it = iter(refs)
    x_ref = next(it)
    g_ref = next(it) if norm else None
    w_ref = next(it)
    res_ref = next(it) if has_res else None
    o_ref = next(it)
    xn_ref = next(it) if emit_normed else None
    x = x_ref[...].astype(F32)
    if norm:
        x = x * lax.rsqrt(jnp.mean(x * x, axis=-1, keepdims=True) + NORM_EPS) * g_ref[...]
        if emit_normed:
            xn_ref[...] = x
    y = jnp.dot(x.astype(BF16), w_ref[...], preferred_element_type=F32)
    if has_res:
        y = y + res_ref[...]
    o_ref[...] = y.astype(o_ref.dtype)


def _linear(x, w, g=None, res=None, emit_normed=False, name="linear"):
    n, k = x.shape
    m = w.shape[1]
    tm = _row_tile(n, 512)
    tn = m if (m <= 2048 or emit_normed) else m // 2
    assert m % tn == 0 and (tn == m or tn % LANES == 0)
    grid = (m // tn, n // tm)
    in_specs = [pl.BlockSpec((tm, k), lambda j, i: (i, 0))]
    args = [x]
    if g is not None:
        in_specs.append(pl.BlockSpec((1, k), lambda j, i: (0, 0)))
        args.append(g.reshape(1, k).astype(F32))
    in_specs.append(pl.BlockSpec((k, tn), lambda j, i: (0, j)))
    args.append(w.astype(BF16))
    if res is not None:
        in_specs.append(pl.BlockSpec((tm, tn), lambda j, i: (i, j)))
        args.append(res)
    out_shape = [jax.ShapeDtypeStruct((n, m), F32)]
    out_specs = [pl.BlockSpec((tm, tn), lambda j, i: (i, j))]
    if emit_normed:
        out_shape.append(jax.ShapeDtypeStruct((n, k), F32))
        out_specs.append(pl.BlockSpec((tm, k), lambda j, i: (i, 0)))
    outs = pl.pallas_call(
        functools.partial(_linear_kernel, norm=g is not None, has_res=res is not None, emit_normed=emit_normed),
        out_shape=out_shape, grid=grid, in_specs=in_specs, out_specs=out_specs,
        compiler_params=_cparams(("arbitrary", "arbitrary")), name=name)(*args)
    return outs if emit_normed else outs[0]


def _outproj_kernel(*refs, diff_heads, diff_scale):
    if diff_heads:
        a0_ref, a1_ref, lam_ref, sub_ref, b_ref, wa_ref, wb_ref, x_ref, o_ref = refs
        lam = lam_ref[0, 0]
        hd = a0_ref.shape[1] // diff_heads
        y = x_ref[...] + jnp.dot(b_ref[...].astype(BF16), wb_ref[...], preferred_element_type=F32)
        for h in range(diff_heads):
            sl = slice(h * hd, (h + 1) * hd)
            o = a0_ref[:, sl] - lam * a1_ref[:, sl]
            o = o * lax.rsqrt(jnp.mean(o * o, axis=-1, keepdims=True) + NORM_EPS) * sub_ref[...] * diff_scale
            y = y + jnp.dot(o.astype(BF16), wa_ref[sl, :], preferred_element_type=F32)
    else:
        a_ref, b_ref, wa_ref, wb_ref, x_ref, o_ref = refs
        y = x_ref[...] + jnp.dot(a_ref[...].astype(BF16), wa_ref[...], preferred_element_type=F32)
        y = y + jnp.dot(b_ref[...].astype(BF16), wb_ref[...], preferred_element_type=F32)
    o_ref[...] = y


def _outproj(a, b, w, x, diff=None, name="outproj"):
    n, d = x.shape
    ka, kb = (a[0].shape[1] if diff else a.shape[1]), b.shape[1]
    tm = _row_tile(n, 512)
    row = lambda i: (i, 0)
    const = lambda i: (0, 0)
    wa, wb = w[:ka].astype(BF16), w[ka:].astype(BF16)
    if diff:
        lam, subln, scale, heads = diff
        args = [a[0], a[1], lam.reshape(1, 1).astype(F32), subln.reshape(1, -1).astype(F32), b, wa, wb, x]
        in_specs = [pl.BlockSpec((tm, ka), row), pl.BlockSpec((tm, ka), row),
                    pl.BlockSpec(memory_space=pltpu.SMEM), pl.BlockSpec((1, ka // heads), const),
                    pl.BlockSpec((tm, kb), row), pl.BlockSpec((ka, d), const), pl.BlockSpec((kb, d), const),
                    pl.BlockSpec((tm, d), row)]
        kern = functools.partial(_outproj_kernel, diff_heads=heads, diff_scale=scale)
    else:
        args = [a, b, wa, wb, x]
        in_specs = [pl.BlockSpec((tm, ka), row), pl.BlockSpec((tm, kb), row), pl.BlockSpec((ka, d), const),
                    pl.BlockSpec((kb, d), const), pl.BlockSpec((tm, d), row)]
        kern = functools.partial(_outproj_kernel, diff_heads=0, diff_scale=1.0)
    return pl.pallas_call(kern, out_shape=jax.ShapeDtypeStruct((n, d), F32), grid=(n // tm,), in_specs=in_specs,
                          out_specs=pl.BlockSpec((tm, d), row), compiler_params=_cparams(("arbitrary",)),
                          name=name)(*args)


def _qk(q, k):
    return lax.dot_general(q, k, (((1,), (1,)), ((), ())), preferred_element_type=F32)


def _softplus(z):
    return jnp.maximum(z, 0.0) + jnp.log(1.0 + jnp.exp(-jnp.abs(z)))


def _split_bf16(x):
    hi = x.astype(BF16)
    return hi, (x - hi.astype(F32)).astype(BF16)


def _lanes(x, width):
    if width <= LANES:
        return x[:, :width]
    return jnp.concatenate([x] * (width // LANES), axis=1)


def _attn_softmax_kernel(q_ref, k_ref, v_ref, o_ref, m_scr, l_scr, acc_scr, *, tq):
    i = pl.program_id(1)
    g, dv = q_ref.shape[1], v_ref.shape[-1]
    r = g * tq
    q = q_ref[0].reshape(r, q_ref.shape[-1])
    m_scr[...] = jnp.full(m_scr.shape, -jnp.inf, F32)
    l_scr[...] = jnp.zeros(l_scr.shape, F32)
    acc_scr[...] = jnp.zeros(acc_scr.shape, F32)

    def chunk(j, masked):
        start = pl.multiple_of(j * tq, tq)
        s = _qk(q, k_ref[0, pl.ds(start, tq), :])
        if masked:
            row = lax.broadcasted_iota(jnp.int32, (r, tq), 0) & (tq - 1)
            col = lax.broadcasted_iota(jnp.int32, (r, tq), 1)
            s = jnp.where(col <= row, s, -jnp.inf)
        m_prev = m_scr[...]
        m_new = jnp.maximum(m_prev, jnp.max(s, axis=1, keepdims=True))
        p = jnp.exp(s - _lanes(m_new, tq))
        alpha = jnp.exp(m_prev - m_new)
        l_scr[...] = alpha * l_scr[...] + jnp.sum(p, axis=1, keepdims=True)
        acc_scr[...] = _lanes(alpha, dv) * acc_scr[...] + jnp.dot(p.astype(BF16), v_ref[0, pl.ds(start, tq), :],
                                                                  preferred_element_type=F32)
        m_scr[...] = m_new

    def body(j, c):
        chunk(j, False)
        return c

    lax.fori_loop(0, i, body, 0)
    chunk(i, True)
    o_ref[0] = (acc_scr[...] / _lanes(l_scr[...], dv)).reshape(g, tq, dv)


def _attn_sb_kernel(q_ref, k_ref, v_ref, tri_ref, o_ref, c_scr, acc_scr, *, tq):
    i = pl.program_id(1)
    g, dv = q_ref.shape[1], v_ref.shape[-1]
    r = g * tq
    q = q_ref[0].reshape(r, q_ref.shape[-1])
    c_scr[...] = jnp.zeros(c_scr.shape, F32)
    acc_scr[...] = jnp.zeros(acc_scr.shape, F32)

    def chunk(j, masked):
        start = pl.multiple_of(j * tq, tq)
        z = _qk(q, k_ref[0, pl.ds(start, tq), :])
        sp = _softplus(z)
        if masked:
            row = lax.broadcasted_iota(jnp.int32, (r, tq), 0) & (tq - 1)
            col = lax.broadcasted_iota(jnp.int32, (r, tq), 1)
            mask = col < row
            sp = jnp.where(mask, sp, 0.0)
        si = jnp.dot(sp.astype(BF16), tri_ref[...], preferred_element_type=F32)
        c_prev = c_scr[...]
        w = jnp.exp(z - si - _lanes(c_prev, tq))
        if masked:
            w = jnp.where(mask, w, 0.0)
        acc_scr[...] += jnp.dot(w.astype(BF16), v_ref[0, pl.ds(start, tq), :], preferred_element_type=F32)
        c_scr[...] = c_prev + jnp.sum(sp, axis=1, keepdims=True)

    chunk(i, True)

    def body(jj, c):
        chunk(i - 1 - jj, False)
        return c

    lax.fori_loop(0, i, body, 0)
    o_ref[0] = acc_scr[...].reshape(g, tq, dv)


def _suffix_ones(n):
    return (lax.broadcasted_iota(jnp.int32, (n, n), 0) >= lax.broadcasted_iota(jnp.int32, (n, n), 1)).astype(BF16)


def _causal_attention(q, k, v, mode, tq, name):
    bh, g, t, dk = q.shape
    dv = v.shape[-1]
    kdiv, vdiv = bh // k.shape[0], bh // v.shape[0]
    tq = min(tq, t)
    assert t % tq == 0 and tq & (tq - 1) == 0
    r = g * tq
    in_specs = [pl.BlockSpec((1, g, tq, dk), lambda b, i: (b, 0, i, 0)),
                pl.BlockSpec((1, t, dk), lambda b, i: (b // kdiv, 0, 0)),
                pl.BlockSpec((1, t, dv), lambda b, i: (b // vdiv, 0, 0))]
    args = [q, k, v]
    if mode == "sb":
        in_specs.append(pl.BlockSpec((tq, tq), lambda b, i: (0, 0)))
        args.append(_suffix_ones(tq))
        kern = functools.partial(_attn_sb_kernel, tq=tq)
        scratch = [pltpu.VMEM((r, LANES), F32), pltpu.VMEM((r, dv), F32)]
    else:
        kern = functools.partial(_attn_softmax_kernel, tq=tq)
        scratch = [pltpu.VMEM((r, LANES), F32), pltpu.VMEM((r, LANES), F32), pltpu.VMEM((r, dv), F32)]
    return pl.pallas_call(kern, out_shape=jax.ShapeDtypeStruct((bh, g, t, dv), F32), grid=(bh, t // tq),
                          in_specs=in_specs, out_specs=pl.BlockSpec((1, g, tq, dv), lambda b, i: (b, 0, i, 0)),
                          scratch_shapes=scratch, compiler_params=_cparams(("arbitrary", "arbitrary")),
                          name=name)(*args)


def _pool_specs(npages, tail, layer, nchunks, nseq, u, reverse=False):
    specs = []
    zeros = (0,) * len(tail)
    for p in range(npages):
        if reverse:
            imap = lambda b, c, pt, p=p: (layer, pt[b * nseq + u, (nchunks - 1 - c) * npages + p]) + zeros
        else:
            imap = lambda b, c, pt, p=p: (layer, pt[b * nseq + u, c * npages + p]) + zeros
        specs.append(pl.BlockSpec((None, None) + tail, imap))
    return specs


def _softmax_init(m_scr, l_scr, acc_scr):
    m_scr[...] = jnp.full(m_scr.shape, -jnp.inf, F32)
    l_scr[...] = jnp.zeros(l_scr.shape, F32)
    acc_scr[...] = jnp.zeros(acc_scr.shape, F32)


def _softmax_update(s, v, m_scr, l_scr, acc_scr):
    m_prev = m_scr[...]
    m_new = jnp.maximum(m_prev, jnp.max(s, axis=1, keepdims=True))
    p = jnp.exp(s - m_new)
    alpha = jnp.exp(m_prev - m_new)
    l_scr[...] = alpha * l_scr[...] + jnp.sum(p, axis=1, keepdims=True)
    acc_scr[...] = alpha * acc_scr[...] + jnp.dot(p.astype(BF16), v, preferred_element_type=F32)
    m_scr[...] = m_new


def _softmax_finish(s_new, v_new, o_ref, m_scr, l_scr, acc_scr):
    m_prev = m_scr[...]
    m_fin = jnp.maximum(m_prev, s_new)
    p_new = jnp.exp(s_new - m_fin)
    alpha = jnp.exp(m_prev - m_fin)
    o_ref[...] = (alpha * acc_scr[...] + p_new * v_new) / (alpha * l_scr[...] + p_new)


def _rows(lo, n):
    return slice(lo * n, (lo + 1) * n)


def _decode_kernel(pt_ref, *refs, seq_fn, npages, nseq, nrow, npool, nextra):
    del pt_ref
    rows, pos = refs[:nrow], nrow
    pages = [[refs[pos + (a * nseq + u) * npages: pos + (a * nseq + u + 1) * npages] for a in range(npool)]
             for u in range(nseq)]
    pos += npool * nseq * npages
    extra, o_ref, scratch = refs[pos:pos + nextra], refs[pos + nextra], refs[pos + nextra + 1:]
    c = pl.program_id(1)
    for u in range(nseq):
        seq_fn(c, c == pl.num_programs(1) - 1, [x.at[u] for x in rows], pages[u], extra, o_ref.at[u],
               [x.at[u] for x in scratch])


def _mla_decode_seq(c, last, rows, pages, extra, o_ref, scratch):
    qa_ref, qp_ref, cnew_ref, pnew_ref = rows
    ckv_pages, kpe_pages = pages
    cb, pb, m_scr, l_scr, acc_scr = scratch
    pl.when(c == 0)(functools.partial(_softmax_init, m_scr, l_scr, acc_scr))
    for p in range(len(ckv_pages)):
        cb[_rows(p, PAGE_SIZE), :] = ckv_pages[p][...].astype(BF16)
        pb[:, _rows(p, PAGE_SIZE)] = kpe_pages[p][...].astype(BF16)
    s = _qk(qa_ref[...], cb[...]) + jnp.dot(qp_ref[...], pb[...], preferred_element_type=F32)
    _softmax_update(s, cb[...], m_scr, l_scr, acc_scr)

    @pl.when(last)
    def _():
        s_new = (jnp.sum(qa_ref[...].astype(F32) * cnew_ref[...], axis=1, keepdims=True)
                 + jnp.sum(qp_ref[...].astype(F32) * pnew_ref[...], axis=1, keepdims=True))
        _softmax_finish(s_new, cnew_ref[...], o_ref, m_scr, l_scr, acc_scr)


def _diff_decode_seq(c, last, rows, pages, extra, o_ref, scratch, *, kv_heads):
    q_ref, knew_ref, vnew_ref = rows
    k_pages, v_pages = pages
    kb, vb, m_scr, l_scr, acc_scr = scratch
    pl.when(c == 0)(functools.partial(_softmax_init, m_scr, l_scr, acc_scr))
    ve = vb.shape[1] // kv_heads
    for p in range(len(k_pages)):
        kb[:, _rows(p, PAGE_SIZE)] = k_pages[p][...].astype(BF16)
        for h in range(kv_heads):
            vb[_rows(p, PAGE_SIZE), _rows(h, ve)] = v_pages[p][pl.ds(h, PAGE_SIZE, stride=kv_heads), :].astype(BF16)
    s = jnp.dot(q_ref[...], kb[...], preferred_element_type=F32)
    _softmax_update(s, vb[...], m_scr, l_scr, acc_scr)

    @pl.when(last)
    def _():
        s_new = jnp.sum(q_ref[...].astype(F32) * knew_ref[...], axis=1, keepdims=True)
        _softmax_finish(s_new, vnew_ref[...], o_ref, m_scr, l_scr, acc_scr)


def _sb_decode_seq(c, last, rows, pages, extra, o_ref, scratch):
    (q_ref,), (tri_ref,) = rows, extra
    k_pages, v_pages = pages
    kb, vb, c_scr, acc_scr = scratch
    npages = len(k_pages)
    r = q_ref.shape[0]

    @pl.when(c == 0)
    def _():
        c_scr[...] = jnp.zeros(c_scr.shape, F32)
        acc_scr[...] = jnp.zeros(acc_scr.shape, F32)

    for p in range(npages):
        kb[:, _rows(p, PAGE_SIZE)] = k_pages[p][...].astype(BF16)
        vb[:, _rows(p, PAGE_SIZE)] = v_pages[p][...].astype(BF16)
    z = jnp.dot(q_ref[...], kb[...], preferred_element_type=F32)
    z_st = jnp.concatenate([z[:, _rows(p, PAGE_SIZE)] for p in range(npages)], axis=0)
    sit = jnp.dot(_softplus(z_st).astype(BF16), tri_ref[...], preferred_element_type=F32)
    si, tot = sit[:, :PAGE_SIZE], sit[:, PAGE_SIZE:]
    run = c_scr[...]
    later = [None] * npages
    for p in range(npages - 1, -1, -1):
        later[p] = run
        run = run + tot[_rows(p, r)]
    c_scr[...] = run
    w_st = jnp.exp(z_st - si - jnp.concatenate(later, axis=0))
    w = jnp.concatenate([w_st[_rows(p, r)] for p in range(npages)], axis=1)
    acc_scr[...] += _qk(w.astype(BF16), vb[...])

    @pl.when(last)
    def _():
        o_ref[...] = acc_scr[...]


def _decode_call(seq_fn, page_table, row_args, pools, extra, r, dv, scratch, name):
    b, n_pages = page_table.shape
    npages = math.gcd(DECODE_PAGES_PER_STEP, n_pages)
    nseq = math.gcd(DECODE_SEQS_PER_STEP, b)
    nchunks = n_pages // npages
    args = list(row_args)
    in_specs = [pl.BlockSpec((nseq,) + a.shape[1:], lambda i, c, pt: (i, 0, 0)) for a in row_args]
    for arr, tail, layer, reverse in pools:
        args += [arr] * (nseq * npages)
        for u in range(nseq):
            in_specs += _pool_specs(npages, tail, layer, nchunks, nseq, u, reverse)
    for arr in extra:
        args.append(arr)
        in_specs.append(pl.BlockSpec(arr.shape, lambda i, c, pt: (0, 0)))
    grid_spec = pltpu.PrefetchScalarGridSpec(
        num_scalar_prefetch=1, grid=(b // nseq, nchunks), in_specs=in_specs,
        out_specs=pl.BlockSpec((nseq, r, dv), lambda i, c, pt: (i, 0, 0)),
        scratch_shapes=[pltpu.VMEM((nseq,) + shape, dt) for shape, dt in scratch(npages * PAGE_SIZE)])
    kern = functools.partial(_decode_kernel, seq_fn=seq_fn, npages=npages, nseq=nseq, nrow=len(row_args),
                             npool=len(pools), nextra=len(extra))
    return pl.pallas_call(kern, out_shape=jax.ShapeDtypeStruct((b, r, dv), F32), grid_spec=grid_spec,
                          compiler_params=_cparams(("arbitrary", "arbitrary")), name=name)(page_table, *args)


def _softmax_scratch(r, dv):
    return [((r, 1), F32), ((r, 1), F32), ((r, dv), F32)]


def _headwise_kernel(x_ref, w_ref, o_ref):
    h, din, dout = w_ref.shape
    for i in range(h):
        o_ref[:, i * dout:(i + 1) * dout] = jnp.dot(x_ref[:, i * din:(i + 1) * din].astype(BF16), w_ref[i],
                                                    preferred_element_type=F32)


def _headwise(x, w, name):
    n = x.shape[0]
    h, din, dout = w.shape
    return pl.pallas_call(_headwise_kernel, out_shape=jax.ShapeDtypeStruct((n, h * dout), F32),
                          name=name)(x, w.astype(BF16))


def _moe_kernel(x_ref, g_ref, wrh_ref, wrl_ref, br_ref, wg_ref, wu_ref, wd_ref, fg_ref, o_ref,
                xn_scr, comb_scr, acc_scr, *, final_norm):
    e = pl.program_id(1)
    tm = x_ref.shape[0]
    lane = lax.broadcasted_iota(jnp.int32, (tm, LANES), 1)

    @pl.when(e == 0)
    def _():
        x = x_ref[...]
        xn = x * lax.rsqrt(jnp.mean(x * x, axis=-1, keepdims=True) + NORM_EPS) * g_ref[...]
        xh, xl = _split_bf16(xn)
        xn_scr[...] = xh
        wrh = wrh_ref[...]
        lg = (jnp.dot(xh, wrh, preferred_element_type=F32) + jnp.dot(xl, wrh, preferred_element_type=F32)
              + jnp.dot(xh, wrl_ref[...], preferred_element_type=F32) + br_ref[...])
        big = jnp.int32(LANES)
        is_g = jnp.logical_and(lane >= N_EXPERTS, lane < N_EXPERTS + N_GROUPS)
        gl = jnp.where(is_g, lg, -jnp.inf)
        gmax = jnp.max(gl, axis=1, keepdims=True)
        gw = 1.0 / jnp.sum(jnp.exp(gl - gmax), axis=1, keepdims=True)
        gsel = jnp.min(jnp.where(gl == gmax, lane, big), axis=1, keepdims=True) - N_EXPERTS
        in_grp = jnp.logical_and(lane < N_EXPERTS, (lane >> 3) == gsel)
        el = jnp.where(in_grp, lg, -jnp.inf)
        v1 = jnp.max(el, axis=1, keepdims=True)
        i1 = jnp.min(jnp.where(el == v1, lane, big), axis=1, keepdims=True)
        el2 = jnp.where(lane == i1, -jnp.inf, el)
        v2 = jnp.max(el2, axis=1, keepdims=True)
        i2 = jnp.min(jnp.where(el2 == v2, lane, big), axis=1, keepdims=True)
        e2 = jnp.exp(v2 - v1)
        w1 = gw / (1.0 + e2)
        comb_scr[...] = jnp.where(lane == i1, w1, 0.0) + jnp.where(lane == i2, w1 * e2, 0.0)
        acc_scr[...] = jnp.zeros(acc_scr.shape, F32)

    xb = xn_scr[...]
    eb = wg_ref.shape[0]
    down = None
    for j in range(eb):
        h = jnp.dot(xb, wg_ref[j], preferred_element_type=F32)
        u = jnp.dot(xb, wu_ref[j], preferred_element_type=F32)
        ce = jnp.sum(jnp.where(lane == e * eb + j, comb_scr[...], 0.0), axis=1, keepdims=True)
        act = h * (1.0 / (1.0 + jnp.exp(-h))) * u * ce
        d = jnp.dot(act.astype(BF16), wd_ref[j], preferred_element_type=F32)
        down = d if down is None else down + d
    acc_scr[...] += down

    @pl.when(e == pl.num_programs(1) - 1)
    def _():
        y = x_ref[...] + acc_scr[...]
        if final_norm:
            y = y * lax.rsqrt(jnp.mean(y * y, axis=-1, keepdims=True) + NORM_EPS) * fg_ref[...]
        o_ref[...] = y


def _moe(x, norm_g, w_group, b_group, w_router, b_router, w_gate, w_up, w_down, final_g, name):
    n, d = x.shape
    ne, _, f = w_gate.shape
    tm = _row_tile(n, 1024)
    eb = math.gcd(MOE_EXPERTS_PER_STEP, ne)
    wr = jnp.zeros((d, LANES), F32)
    wr = wr.at[:, :N_EXPERTS].set(w_router.reshape(d, N_EXPERTS)).at[:, N_EXPERTS:N_EXPERTS + N_GROUPS].set(w_group)
    br = jnp.zeros((1, LANES), F32)
    br = br.at[0, :N_EXPERTS].set(b_router.reshape(N_EXPERTS)).at[0, N_EXPERTS:N_EXPERTS + N_GROUPS].set(b_group)
    wrh = wr.astype(BF16)
    wrl = (wr - wrh.astype(F32)).astype(BF16)
    fg = jnp.ones((1, d), F32) if final_g is None else final_g.reshape(1, d).astype(F32)
    row = lambda i, e: (i, 0)
    const = lambda i, e: (0, 0)
    in_specs = [pl.BlockSpec((tm, d), row), pl.BlockSpec((1, d), const), pl.BlockSpec((d, LANES), const),
                pl.BlockSpec((d, LANES), const), pl.BlockSpec((1, LANES), const),
                pl.BlockSpec((eb, d, f), lambda i, e: (e, 0, 0)), pl.BlockSpec((eb, d, f), lambda i, e: (e, 0, 0)),
                pl.BlockSpec((eb, f, d), lambda i, e: (e, 0, 0)), pl.BlockSpec((1, d), const)]
    return pl.pallas_call(
        functools.partial(_moe_kernel, final_norm=final_g is not None),
        out_shape=jax.ShapeDtypeStruct((n, d), F32), grid=(n // tm, ne // eb), in_specs=in_specs,
        out_specs=pl.BlockSpec((tm, d), row),
        scratch_shapes=[pltpu.VMEM((tm, d), BF16), pltpu.VMEM((tm, LANES), F32), pltpu.VMEM((tm, d), F32)],
        compiler_params=_cparams(("arbitrary", "arbitrary")), name=name,
    )(x, norm_g.reshape(1, d).astype(F32), wrh, wrl, br, w_gate.astype(BF16), w_up.astype(BF16),
      w_down.astype(BF16), fg)


def _rwkv_slot(s, idx, lane, r_ref, w_ref, k_ref, kk_ref, kka_ref, vt_ref, y_scr, c):
    row = lambda ref: ref[c, pl.ds(idx, 1), :]
    sel = lane == idx
    vcol = jnp.sum(jnp.where(sel, vt_ref[c], 0.0), axis=1, keepdims=True)
    sa = -jnp.sum(s * row(kk_ref), axis=1, keepdims=True)
    s = s * row(w_ref) + sa * row(kka_ref) + vcol * row(k_ref)
    y = jnp.sum(s * row(r_ref), axis=1, keepdims=True)
    y_scr[c] = jnp.where(sel, y, y_scr[c])
    return s


def _rwkv_scan_kernel(r_ref, w_ref, k_ref, kk_ref, kka_ref, vt_ref, yt_ref, sfin_ref, s_scr, y_scr, wrb_scr,
                      gamma_scr):
    tblk = pl.program_id(0)
    nch, tb, hs = r_ref.shape

    @pl.when(tblk == 0)
    def _():
        s_scr[...] = jnp.zeros(s_scr.shape, F32)

    lane = lax.broadcasted_iota(jnp.int32, (1, hs, tb), 2)
    r_blk = r_ref[...]
    beta = jnp.sum(kka_ref[...] * r_blk, axis=2, keepdims=True)
    wrb_scr[...] = w_ref[...] * r_blk - beta * kk_ref[...]
    gamma_scr[...] = jnp.broadcast_to(jnp.sum(k_ref[...] * r_blk, axis=2, keepdims=True), gamma_scr.shape)
    pick_row = lax.broadcasted_iota(jnp.int32, (tb, 2 * tb), 0)
    pick_col = lax.broadcasted_iota(jnp.int32, (tb, 2 * tb), 1)

    def step(t, vcol):
        row = lambda ref: ref[:, pl.ds(t, 1), :]
        s = s_scr[...]
        sa = -jnp.sum(s * row(kk_ref), axis=2, keepdims=True)
        y0 = jnp.sum(s * row(wrb_scr), axis=2, keepdims=True)
        s_scr[...] = s * row(w_ref) + sa * row(kka_ref) + vcol[:, :, :hs] * row(k_ref)
        pltpu.store(y_scr, y0 + vcol * row(gamma_scr), mask=jnp.broadcast_to(lane == t, y_scr.shape))

    def pair(p, carry):
        t0 = 2 * p
        pick = jnp.where(pick_row == t0 + (pick_col >= tb).astype(jnp.int32), 1.0, 0.0).astype(BF16)
        vb = jnp.dot(vt_ref[...].reshape(nch * hs, tb), pick, preferred_element_type=F32).reshape(nch, hs, 2 * tb)
        step(t0, vb[:, :, :tb])
        step(t0 + 1, vb[:, :, tb:])
        return carry

    lax.fori_loop(0, tb // 2, pair, 0)
    yt_ref[...] = y_scr[...]

    @pl.when(tblk == pl.num_programs(0) - 1)
    def _():
        sfin_ref[...] = s_scr[...]


def _rwkv_scan(r, w, k, kk, kka, vt):
    nch, t, hs = r.shape
    tb = min(t, LANES)
    assert t % tb == 0
    rows = pl.BlockSpec((nch, tb, hs), lambda i: (0, i, 0))
    cols = pl.BlockSpec((nch, hs, tb), lambda i: (0, 0, i))
    return pl.pallas_call(
        _rwkv_scan_kernel,
        out_shape=[jax.ShapeDtypeStruct((nch, hs, t), F32), jax.ShapeDtypeStruct((nch, hs, hs), F32)],
        grid=(t // tb,), in_specs=[rows] * 5 + [cols],
        out_specs=[cols, pl.BlockSpec((nch, hs, hs), lambda i: (0, 0, 0))],
        scratch_shapes=[pltpu.VMEM((nch, hs, hs), F32), pltpu.VMEM((nch, hs, tb), F32),
                        pltpu.VMEM((nch, tb, hs), F32), pltpu.VMEM((nch, tb, tb), F32)],
        compiler_params=_cparams(("arbitrary",)), name="rwkv_scan")(r, w, k, kk, kka, vt)


def _rwkv_step_kernel(s_ref, r_ref, w_ref, k_ref, kk_ref, kka_ref, vt_ref, so_ref, yt_ref, y_scr):
    nb, hs = r_ref.shape[1], r_ref.shape[2]
    lane = lax.broadcasted_iota(jnp.int32, (hs, nb), 1)
    y_scr[...] = jnp.zeros(y_scr.shape, F32)

    def step(b, carry):
        so_ref[b, 0] = _rwkv_slot(s_ref[b, 0], b, lane, r_ref, w_ref, k_ref, kk_ref, kka_ref, vt_ref, y_scr, 0)
        return carry

    lax.fori_loop(0, nb, step, 0)
    yt_ref[...] = y_scr[...]


def _rwkv_step(s, r, w, k, kk, kka, vt):
    b, h, hs, _ = s.shape
    nb = min(b, LANES)
    assert b % nb == 0
    sspec = pl.BlockSpec((nb, 1, hs, hs), lambda j, i: (i, j, 0, 0))
    rows = pl.BlockSpec((1, nb, hs), lambda j, i: (j, i, 0))
    cols = pl.BlockSpec((1, hs, nb), lambda j, i: (j, 0, i))
    return pl.pallas_call(
        _rwkv_step_kernel,
        out_shape=[jax.ShapeDtypeStruct(s.shape, F32), jax.ShapeDtypeStruct((h, hs, b), F32)],
        grid=(h, b // nb), in_specs=[sspec] + [rows] * 5 + [cols], out_specs=[sspec, cols],
        scratch_shapes=[pltpu.VMEM((1, hs, nb), F32)],
        compiler_params=_cparams(("arbitrary", "arbitrary")), name="rwkv_step")(s, r, w, k, kk, kka, vt)


def _rope_tables(pos, reps):
    half = MLA_ROPE // 2
    inv = ROPE_BASE ** (-jnp.arange(half, dtype=F32) / half)
    ang = pos.astype(F32)[:, None] * inv[None, :]
    cos, sin = jnp.cos(ang), jnp.sin(ang)
    return jnp.tile(jnp.concatenate([cos, cos], -1), (1, reps)), jnp.tile(jnp.concatenate([-sin, sin], -1), (1, reps))


def _swap_halves(w, width):
    k, m = w.shape
    w = w.reshape(k, m // width, 2, width // 2)
    return jnp.concatenate([w[:, :, 1], w[:, :, 0]], axis=2).reshape(k, m)


def _even_weights(w_in, q_norm, w_uq, kv_norm, w_uk, w_uv, w_out):
    kpe_lo = MLA_Q_LORA + MLA_KV_LORA
    w_in_ext = jnp.concatenate([w_in, _swap_halves(w_in[:, kpe_lo:kpe_lo + MLA_ROPE], MLA_ROPE)], axis=1)
    wq = w_uq.reshape(MLA_Q_LORA, MLA_HEADS, MLA_NOPE + MLA_ROPE)
    wq_n = wq[:, :, :MLA_NOPE].reshape(MLA_Q_LORA, -1)
    wq_p = wq[:, :, MLA_NOPE:].reshape(MLA_Q_LORA, -1)
    wq_ext = jnp.concatenate([wq_n, wq_p, _swap_halves(wq_p, MLA_ROPE)], axis=1)
    w_kv = jnp.concatenate([w_uk.reshape(MLA_KV_LORA, -1), w_uv.reshape(MLA_KV_LORA, -1)], axis=1)
    return dict(w_in=w_in_ext, q_norm=q_norm, wq=wq_ext, kv_norm=kv_norm, w_kv=w_kv, w_uk=w_uk, w_uv=w_uv,
                w_out=w_out)


def _even_project(h_in, attn_g, wts, pos, b, t):
    n = h_in.shape[0]
    proj = _linear(h_in, wts["w_in"], g=attn_g, name="even_in")
    o = 0
    parts = []
    for wdt in (MLA_Q_LORA, MLA_KV_LORA, MLA_ROPE, SB_HEADS * SB_HD, SB_KV_HEADS * SB_HD, SB_KV_HEADS * SB_HD,
                MLA_ROPE):
        parts.append(proj[:, o:o + wdt])
        o += wdt
    q_lat, kv_lat, kpe_raw, sq, sk, sv, kpe_sw = parts
    cos1, sin1 = _rope_tables(pos, 1)
    cos1, sin1 = jnp.tile(cos1, (b, 1)), jnp.tile(sin1, (b, 1))
    kpe = kpe_raw * cos1 + kpe_sw * sin1
    qall = _linear(q_lat, wts["wq"], g=wts["q_norm"], name="mla_q_up")
    hp = MLA_HEADS * MLA_ROPE
    q_nope, q_pe, q_pe_sw = qall[:, :MLA_HEADS * MLA_NOPE], qall[:, -2 * hp:-hp], qall[:, -hp:]
    q_pe = q_pe * jnp.tile(cos1, (1, MLA_HEADS)) + q_pe_sw * jnp.tile(sin1, (1, MLA_HEADS))
    return q_nope, q_pe, kv_lat, kpe, sq, sk, sv


def _heads_first(x, b, t, h):
    d = x.shape[1] // h
    return x.reshape(b, t, h, d).transpose(0, 2, 1, 3).reshape(b * h, t, d)


def _even_prompt(x, attn_g, wts, b, t):
    pos = jnp.arange(t, dtype=jnp.int32)
    q_nope, q_pe, kv_lat, kpe, sq, sk, sv = _even_project(x, attn_g, wts, pos, b, t)
    kexp, ckv = _linear(kv_lat, wts["w_kv"], g=wts["kv_norm"], emit_normed=True, name="mla_kv_up")
    hn = MLA_HEADS * MLA_NOPE
    scale = (MLA_NOPE + MLA_ROPE) ** -0.5
    q = jnp.concatenate([_heads_first(q_nope, b, t, MLA_HEADS), _heads_first(q_pe, b, t, MLA_HEADS)], axis=-1)
    q = (q * scale).astype(BF16)[:, None]
    k = jnp.concatenate([_heads_first(kexp[:, :hn], b, t, MLA_HEADS),
                         jnp.broadcast_to(kpe.reshape(b, 1, t, MLA_ROPE), (b, MLA_HEADS, t, MLA_ROPE)
                                          ).reshape(b * MLA_HEADS, t, MLA_ROPE)], axis=-1).astype(BF16)
    v = _heads_first(kexp[:, hn:], b, t, MLA_HEADS).astype(BF16)
    mla = _causal_attention(q, k, v, "softmax", 512, "mla_prompt")
    mla = mla.reshape(b, MLA_HEADS, t, MLA_V).transpose(0, 2, 1, 3).reshape(b * t, MLA_HEADS * MLA_V)
    grp = SB_HEADS // SB_KV_HEADS
    sq_h = (sq * SB_HD ** -0.5).astype(BF16).reshape(b, t, SB_KV_HEADS, grp, SB_HD).transpose(0, 2, 3, 1, 4)
    sq_h = sq_h.reshape(b * SB_KV_HEADS, grp, t, SB_HD)
    sbo = _causal_attention(sq_h, _heads_first(sk, b, t, SB_KV_HEADS).astype(BF16),
                            _heads_first(sv, b, t, SB_KV_HEADS).astype(BF16), "sb", 256, "sb_prompt")
    sbo = sbo.reshape(b, SB_KV_HEADS, grp, t, SB_HD).transpose(0, 3, 1, 2, 4).reshape(b * t, SB_HEADS * SB_HD)
    y = _outproj(mla, sbo, wts["w_out"], x, name="even_out")
    return y, (ckv, kpe, sk, sv)


def _feature_major(cache):
    nl, npool, page = cache.shape[:3]
    nd = cache.ndim
    return cache.transpose((0, 1) + tuple(range(3, nd)) + (2,)).reshape(nl, npool, -1, page)


def _even_sample(x, attn_g, wts, page_table, layer, c_ckv, c_kpe, c_sk, c_sv):
    b = x.shape[0]
    past_len = page_table.shape[1] * PAGE_SIZE
    pos = jnp.full((1,), past_len, jnp.int32)
    q_nope, q_pe, kv_lat, kpe, sq, sk, sv = _even_project(x, attn_g, wts, pos, b, 1)
    _, ckv = _linear(kv_lat, wts["w_kv"], g=wts["kv_norm"], emit_normed=True, name="mla_kv_norm_s")
    scale = (MLA_NOPE + MLA_ROPE) ** -0.5
    q_abs = _headwise(q_nope, wts["w_uk"].transpose(1, 2, 0), "mla_q_abs")
    q_abs = (q_abs * scale).astype(BF16).reshape(b, MLA_HEADS, MLA_KV_LORA)
    q_pe = (q_pe * scale).astype(BF16).reshape(b, MLA_HEADS, MLA_ROPE)
    lat = _decode_call(
        _mla_decode_seq, page_table, [q_abs, q_pe, ckv[:, None], kpe[:, None]],
        [(c_ckv, (PAGE_SIZE, MLA_KV_LORA), layer, False), (_feature_major(c_kpe), (MLA_ROPE, PAGE_SIZE), layer, False)],
        [], MLA_HEADS, MLA_KV_LORA,
        lambda tk: [((tk, MLA_KV_LORA), BF16), ((MLA_ROPE, tk), BF16)] + _softmax_scratch(MLA_HEADS, MLA_KV_LORA),
        "mla_decode")
    mla = _headwise(lat.reshape(b, MLA_HEADS * MLA_KV_LORA), wts["w_uv"].transpose(1, 0, 2), "mla_v_up")
    kvw = SB_KV_HEADS * SB_HD
    grp = SB_HEADS // SB_KV_HEADS
    sq_rows = (sq * SB_HD ** -0.5).reshape(b, SB_KV_HEADS, grp, 1, SB_HD) * jnp.eye(SB_KV_HEADS, dtype=F32)[
        None, :, None, :, None]
    sq_rows = sq_rows.reshape(b, SB_HEADS, kvw).astype(BF16)
    tri = jnp.concatenate([_suffix_ones(PAGE_SIZE), jnp.ones((PAGE_SIZE, PAGE_SIZE), BF16)], axis=1)
    sbo = _decode_call(
        _sb_decode_seq, page_table, [sq_rows],
        [(_feature_major(c_sk), (kvw, PAGE_SIZE), layer, True), (_feature_major(c_sv), (kvw, PAGE_SIZE), layer, True)],
        [tri], SB_HEADS, kvw,
        lambda tk: [((kvw, tk), BF16), ((kvw, tk), BF16), ((SB_HEADS, LANES), F32), ((SB_HEADS, kvw), F32)],
        "sb_decode")
    sbo = sbo.reshape(b, SB_KV_HEADS, grp, SB_KV_HEADS, SB_HD)
    sbo = jnp.einsum("bkgjd,kj->bkgd", sbo, jnp.eye(SB_KV_HEADS, dtype=F32)).reshape(b, SB_HEADS * SB_HD)
    y = _outproj(mla, sbo, wts["w_out"], x, name="even_out_s")
    return y, (ckv, kpe, sk, sv)


def _rwkv_prepare(rw, prev, p):
    xs = rw + (prev - rw) * p["mu"]
    o = 0
    parts = []
    for wdt in (RW_DIM, RW_DIM, RW_DIM, RW_W_LORA, RW_A_LORA, RW_G_LORA):
        parts.append(xs[:, o:o + wdt])
        o += wdt
    r, k, v, xw, xa, xg = parts
    w_log = -jax.nn.softplus(-(p["w0"] + _linear(jnp.tanh(xw), p["w2"], name="rw_w_lora"))) - 0.5
    decay = jnp.exp(-jnp.exp(w_log))
    a = jax.nn.sigmoid(p["a0"] + _linear(xa, p["a2"], name="rw_a_lora"))
    g = _linear(jax.nn.sigmoid(xg), p["g2"], name="rw_g_lora")
    n = r.shape[0]
    kk = (k * p["kk"]).reshape(n, RW_HEADS, RW_HS)
    kk = (kk * lax.rsqrt(jnp.maximum(jnp.sum(kk * kk, axis=-1, keepdims=True), 1e-24))).reshape(n, RW_DIM)
    k = k * (1.0 + (a - 1.0) * p["ka"])
    return r, decay, k, v, kk, kk * a, g


def _rwkv_finish(yr, r, k, v, g, p):
    n = yr.shape[0]
    yh = yr.reshape(n, RW_HEADS, RW_HS)
    mean = jnp.mean(yh, axis=-1, keepdims=True)
    var = jnp.mean(jnp.square(yh - mean), axis=-1, keepdims=True)
    yn = ((yh - mean) * lax.rsqrt(var + RW_LN_EPS)).reshape(n, RW_DIM) * p["ln_w"] + p["ln_b"]
    rk = (r * k * p["rk"].reshape(1, RW_DIM)).reshape(n, RW_HEADS, RW_HS)
    bonus = (jnp.sum(rk, axis=-1, keepdims=True) * v.reshape(n, RW_HEADS, RW_HS)).reshape(n, RW_DIM)
    return (yn + bonus) * g


def _odd_split(proj):
    dq_w = DF_HEADS * 2 * DF_HD
    dkv_w = DF_KV_HEADS * 2 * DF_HD
    return proj[:, :dq_w], proj[:, dq_w:dq_w + dkv_w], proj[:, dq_w + dkv_w:dq_w + 2 * dkv_w], proj[:, dq_w + 2 * dkv_w:]


def _odd_prompt(x, attn_g, p, lam, lam_init, b, t):
    n = x.shape[0]
    proj = _linear(x, p["w_in"], g=attn_g, name="odd_in")
    dq, dk, dv, rw = _odd_split(proj)
    grp = DF_HEADS // DF_KV_HEADS
    q = (dq * DF_HD ** -0.5).astype(BF16).reshape(b, t, DF_KV_HEADS, grp, 2, DF_HD).transpose(0, 2, 4, 3, 1, 5)
    q = q.reshape(b * DF_KV_HEADS * 2, grp, t, DF_HD)
    k = dk.astype(BF16).reshape(b, t, DF_KV_HEADS * 2, DF_HD).transpose(0, 2, 1, 3).reshape(-1, t, DF_HD)
    v = _heads_first(dv, b, t, DF_KV_HEADS).astype(BF16)
    o = _causal_attention(q, k, v, "softmax", 512, "diff_prompt")
    o = o.reshape(b, DF_KV_HEADS, 2, grp, t, 2 * DF_HD).transpose(2, 0, 4, 1, 3, 5).reshape(2, n, DF_HEADS * 2 * DF_HD)
    rwb = rw.reshape(b, t, -1)
    prev = jnp.concatenate([jnp.zeros((b, 1, rwb.shape[-1]), F32), rwb[:, :-1]], axis=1).reshape(n, -1)
    r, decay, k_, v_, kk, kka, g = _rwkv_prepare(rw, prev, p)
    hf = lambda z: _heads_first(z, b, t, RW_HEADS)
    vt = hf(v_).transpose(0, 2, 1).astype(BF16)
    yt, s_fin = _rwkv_scan(hf(r), hf(decay), hf(k_), hf(kk), hf(kka), vt)
    yr = yt.reshape(b, RW_HEADS, RW_HS, t).transpose(0, 3, 1, 2).reshape(n, RW_DIM)
    rw_out = _rwkv_finish(yr, r, k_, v_, g, p)
    y = _outproj((o[0], o[1]), rw_out, p["w_out"], x, diff=(lam, p["subln"], 1.0 - lam_init, DF_HEADS),
                 name="odd_out")
    s_fin = s_fin.reshape(b, RW_HEADS, RW_HS, RW_HS)
    return y, (dk, dv), s_fin, rwb[:, -1]


def _odd_sample(x, attn_g, p, lam, lam_init, page_table, layer, c_dk, c_dv, shift_prev, s_prev):
    b = x.shape[0]
    proj = _linear(x, p["w_in"], g=attn_g, name="odd_in_s")
    dq, dk, dv, rw = _odd_split(proj)
    nl, npool = c_dk.shape[:2]
    kw = DF_KV_HEADS * 2 * DF_HD
    vw = DF_KV_HEADS * 2 * DF_HD
    q_rows = _block_rows_kgc(dq * DF_HD ** -0.5).astype(BF16)
    nrow = DF_HEADS * 2
    o = _decode_call(
        functools.partial(_diff_decode_seq, kv_heads=DF_KV_HEADS), page_table, [q_rows, dk[:, None], dv[:, None]],
        [(_feature_major(c_dk), (kw, PAGE_SIZE), layer, False),
         (c_dv.reshape(nl, npool, PAGE_SIZE * DF_KV_HEADS, 2 * DF_HD), (PAGE_SIZE * DF_KV_HEADS, 2 * DF_HD), layer,
          False)],
        [], nrow, vw,
        lambda tk: [((kw, tk), BF16), ((tk, vw), BF16)] + _softmax_scratch(nrow, vw),
        "diff_decode")
    grp = DF_HEADS // DF_KV_HEADS
    o = o.reshape(b, DF_KV_HEADS, grp, 2, DF_KV_HEADS, 2 * DF_HD)
    o = jnp.einsum("bkgcje,kj->cbkge", o, jnp.eye(DF_KV_HEADS, dtype=F32)).reshape(2, b, DF_HEADS * 2 * DF_HD)
    r, decay, k_, v_, kk, kka, g = _rwkv_prepare(rw, shift_prev, p)
    hb = lambda z: z.reshape(b, RW_HEADS, RW_HS).transpose(1, 0, 2)
    s_new, yt = _rwkv_step(s_prev, hb(r), hb(decay), hb(k_), hb(kk), hb(kka), hb(v_).transpose(0, 2, 1))
    yr = yt.transpose(2, 0, 1).reshape(b, RW_DIM)
    rw_out = _rwkv_finish(yr, r, k_, v_, g, p)
    y = _outproj((o[0], o[1]), rw_out, p["w_out"], x, diff=(lam, p["subln"], 1.0 - lam_init, DF_HEADS),
                 name="odd_out_s")
    return y, (dk, dv), s_new, rw


def _block_rows_kgc(dq):
    b = dq.shape[0]
    grp = DF_HEADS // DF_KV_HEADS
    q = dq.reshape(b, DF_KV_HEADS, grp, 2, 1, 1, DF_HD)
    eye_k = jnp.eye(DF_KV_HEADS, dtype=dq.dtype)[None, :, None, None, :, None, None]
    eye_c = jnp.eye(2, dtype=dq.dtype)[None, None, None, :, None, :, None]
    return (q * eye_k * eye_c).reshape(b, DF_KV_HEADS * grp * 2, DF_KV_HEADS * 2 * DF_HD)


def kernel(x_prompt, x_sample, cache_mla_ckv, cache_mla_kpe, cache_sb_k, cache_sb_v, cache_df_k, cache_df_v, state_rw_s, state_rw_shift, page_table, attn_norm, ffn_norm, final_norm, ev_w_in, mla_q_norm, mla_w_uq, mla_kv_norm, mla_w_uk, mla_w_uv, ev_w_out, od_w_in, df_lq1, df_lk1, df_lq2, df_lk2, df_subln, rw_mu, rw_w0, rw_w2, rw_a0, rw_a2, rw_g2, rw_kk, rw_ka, rw_rk, rw_ln_w, rw_ln_b, od_w_out, moe_w_group, moe_b_group, moe_w_router, moe_b_router, moe_w_gate, moe_w_up, moe_w_down):
    bp, tp, d = x_prompt.shape
    bs, ts, _ = x_sample.shape
    assert ts == 1
    depth = attn_norm.shape[0]
    xp = x_prompt.reshape(bp * tp, d)
    xs = x_sample.reshape(bs, d)
    outs = {k: [] for k in ("ckv_p", "ckv_s", "kpe_p", "kpe_s", "sbk_p", "sbk_s", "sbv_p", "sbv_s", "dfk_p", "dfk_s",
                            "dfv_p", "dfv_s", "rws_p", "rws_s", "rwsh_p", "rwsh_s")}
    for layer in range(depth):
        i = layer // 2
        if layer % 2 == 0:
            wts = _even_weights(ev_w_in[i], mla_q_norm[i], mla_w_uq[i], mla_kv_norm[i], mla_w_uk[i], mla_w_uv[i],
                                ev_w_out[i])
            xp, rp = _even_prompt(xp, attn_norm[layer], wts, bp, tp)
            xs, rs = _even_sample(xs, attn_norm[layer], wts, page_table, i, cache_mla_ckv, cache_mla_kpe,
                                  cache_sb_k, cache_sb_v)
            outs["ckv_p"].append(rp[0].reshape(bp, tp, MLA_KV_LORA))
            outs["kpe_p"].append(rp[1].reshape(bp, tp, MLA_ROPE))
            outs["sbk_p"].append(rp[2].reshape(bp, tp, SB_KV_HEADS, SB_HD))
            outs["sbv_p"].append(rp[3].reshape(bp, tp, SB_KV_HEADS, SB_HD))
            outs["ckv_s"].append(rs[0].reshape(bs, ts, MLA_KV_LORA))
            outs["kpe_s"].append(rs[1].reshape(bs, ts, MLA_ROPE))
            outs["sbk_s"].append(rs[2].reshape(bs, ts, SB_KV_HEADS, SB_HD))
            outs["sbv_s"].append(rs[3].reshape(bs, ts, SB_KV_HEADS, SB_HD))
        else:
            lam_init = 0.8 - 0.6 * math.exp(-0.3 * layer)
            lam = (jnp.exp(jnp.sum(df_lq1[i] * df_lk1[i])) - jnp.exp(jnp.sum(df_lq2[i] * df_lk2[i])) + lam_init)
            p = dict(w_in=od_w_in[i], subln=df_subln[i], mu=rw_mu[i], w0=rw_w0[i], w2=rw_w2[i], a0=rw_a0[i],
                     a2=rw_a2[i], g2=rw_g2[i], kk=rw_kk[i], ka=rw_ka[i], rk=rw_rk[i], ln_w=rw_ln_w[i],
                     ln_b=rw_ln_b[i], w_out=od_w_out[i])
            xp, kvp, sp, shp = _odd_prompt(xp, attn_norm[layer], p, lam, lam_init, bp, tp)
            xs, kvs, ss, shs = _odd_sample(xs, attn_norm[layer], p, lam, lam_init, page_table, i, cache_df_k,
                                           cache_df_v, state_rw_shift[i], state_rw_s[i])
            outs["dfk_p"].append(kvp[0].reshape(bp, tp, DF_KV_HEADS, 2, DF_HD))
            outs["dfv_p"].append(kvp[1].reshape(bp, tp, DF_KV_HEADS, 2 * DF_HD))
            outs["dfk_s"].append(kvs[0].reshape(bs, ts, DF_KV_HEADS, 2, DF_HD))
            outs["dfv_s"].append(kvs[1].reshape(bs, ts, DF_KV_HEADS, 2 * DF_HD))
            outs["rws_p"].append(sp)
            outs["rws_s"].append(ss)
            outs["rwsh_p"].append(shp)
            outs["rwsh_s"].append(shs)
        fin = final_norm if layer == depth - 1 else None
        mw = (moe_w_group[layer], moe_b_group[layer], moe_w_router[layer], moe_b_router[layer], moe_w_gate[layer],
              moe_w_up[layer], moe_w_down[layer])
        xp = _moe(xp, ffn_norm[layer], *mw, fin, "moe_p")
        xs = _moe(xs, ffn_norm[layer], *mw, fin, "moe_s")
    st = lambda key: jnp.stack(outs[key], axis=0)
    return (xp.reshape(bp, tp, d), xs.reshape(bs, ts, d), st("ckv_p"), st("ckv_s"), st("kpe_p"), st("kpe_s"),
            st("sbk_p"), st("sbk_s"), st("sbv_p"), st("sbv_s"), st("dfk_p"), st("dfk_s"), st("dfv_p"), st("dfv_s"),
            st("rws_p"), st("rws_s"), st("rwsh_p"), st("rwsh_s"))
```
